```python
import jax, jax.numpy as jnp
from jax import lax
import numpy as np

D_MODEL = 1024
BATCH = 8
SEQ = 2048
DEPTH = 4
DEC_BATCH = 128
DEC_SEQ = 4
PAST_LEN = 16384
PAGE_SIZE = 128

A_HEADS = 8
A_HEAD_DIM = D_MODEL // A_HEADS
A_WIDTH = A_HEADS * A_HEAD_DIM
B_WIDTH = D_MODEL
CONV_W = 3
CHUNK = 64
DEEPNORM_ALPHA = (2 * DEPTH) ** 0.25
DEEPNORM_BETA = (8 * DEPTH) ** -0.25
LN_EPS = 1e-5
RMS_EPS = 1e-6
LB_FLOOR = 1e-30
PROJ_SIZES = (A_WIDTH, A_WIDTH, A_WIDTH, A_WIDTH, B_WIDTH, B_WIDTH, B_WIDTH, B_WIDTH, D_MODEL, D_MODEL)
N_PROJ = sum(PROJ_SIZES)

kernel_name = "hgrn2_shortconv_gated_parallel_deepnorm_step"


def layer_norm(x, g, b):
    xf = x.astype(jnp.float32)
    mu = jnp.mean(xf, axis=-1, keepdims=True)
    var = jnp.mean(jnp.square(xf - mu), axis=-1, keepdims=True)
    return ((xf - mu) * lax.rsqrt(var + LN_EPS) * g.astype(jnp.float32) + b.astype(jnp.float32)).astype(x.dtype)


def hgrn2_lower_bounds(lb_param):
    p = jax.nn.softmax(lb_param.astype(jnp.float32), axis=0)
    return jnp.cumsum(p, axis=0) - p[0]


def hgrn2_recurrence(q, log_f, k, v, s0):
    b, t, h, dk = q.shape
    c = CHUNK if t % CHUNK == 0 else t
    n = t // c

    def to_chunks(a):
        return a.reshape(b, n, c, h, a.shape[-1]).transpose(1, 0, 3, 2, 4)

    causal = jnp.tril(jnp.ones((c, c), dtype=bool))[:, :, None]

    def step(s, inp):
        qi, gi, ki, vi = inp
        g_cum = jnp.cumsum(gi, axis=2)
        o_inter = jnp.einsum('bhtk,bhkv->bhtv', qi * jnp.exp(g_cum), s)
        diff = g_cum[:, :, :, None, :] - g_cum[:, :, None, :, :]
        decay = jnp.where(causal, jnp.exp(jnp.where(causal, diff, 0.0)), 0.0)
        attn = jnp.einsum('bhtk,bhsk,bhtsk->bhts', qi, ki, decay)
        o = o_inter + jnp.einsum('bhts,bhsv->bhtv', attn, vi)
        g_last = g_cum[:, :, -1:, :]
        k_dec = ki * jnp.exp(g_last - g_cum)
        s_new = jnp.exp(g_last[:, :, 0, :])[..., None] * s + jnp.einsum('bhsk,bhsv->bhkv', k_dec, vi)
        return s_new, o

    s_fin, o = lax.scan(step, s0, (to_chunks(q), to_chunks(log_f), to_chunks(k), to_chunks(v)))
    o = o.transpose(1, 0, 3, 2, 4).reshape(b, t, h, v.shape[-1])
    return o, s_fin


def short_conv(u, buf, w):
    t = u.shape[1]
    ext = jnp.concatenate([buf.astype(u.dtype), u], axis=1)
    y = ext[:, 0:t] * w[0]
    for j in range(1, CONV_W):
        y = y + ext[:, j:j + t] * w[j]
    return y, ext[:, -(CONV_W - 1):]


def trunk_layer(x, c, conv_buf, s0, lb, w_ada, b_ada, w_in, gnorm, conv_w, w_a_out, w_b_out, w_o, ln_g, ln_b):
    dt = x.dtype
    bsz, t, _ = x.shape
    ada = jnp.einsum('bd,de->be', jax.nn.silu(c), w_ada) + b_ada
    shift, scale, gate = jnp.split(ada, 3, axis=-1)
    h = x * (1 + scale[:, None, :]) + shift[:, None, :]
    proj = jnp.einsum('btd,de->bte', h, w_in)
    split_at = [int(s) for s in np.cumsum(PROJ_SIZES)[:-1]]
    q, fz, iv, ga, bg, cg, hv, gb, ra, rb = jnp.split(proj, split_at, axis=-1)

    def heads(a):
        return a.reshape(bsz, t, A_HEADS, A_HEAD_DIM).astype(jnp.float32)
    fz32 = heads(fz)
    lb_h = lb.reshape(A_HEADS, A_HEAD_DIM)
    log_f = jnp.logaddexp(jnp.log(jnp.maximum(lb_h, LB_FLOOR)), jnp.log1p(-lb_h) + jax.nn.log_sigmoid(fz32))
    k = (1 - lb_h) * jax.nn.sigmoid(-fz32)
    qh = jax.nn.silu(heads(q)) * (A_HEAD_DIM ** -0.5)
    o, s_new = hgrn2_recurrence(qh, log_f, k, heads(iv), s0.astype(jnp.float32))
    o = o * lax.rsqrt(jnp.mean(o * o, axis=-1, keepdims=True) + RMS_EPS) * gnorm.astype(jnp.float32).reshape(A_HEADS, A_HEAD_DIM)
    ya = jnp.einsum('btw,wd->btd', o.reshape(bsz, t, A_WIDTH).astype(dt) * jax.nn.silu(ga), w_a_out)

    v, buf_new = short_conv(cg * hv, conv_buf, conv_w)
    yb = jnp.einsum('btw,wd->btd', bg * v * jax.nn.silu(gb), w_b_out)

    m = jax.nn.sigmoid(ra) * ya + jax.nn.sigmoid(rb) * yb
    out = jnp.einsum('btd,de->bte', m, w_o)
    x_new = layer_norm(DEEPNORM_ALPHA * x + gate[:, None, :] * out, ln_g, ln_b)
    return x_new, buf_new, s_new


def run_trunk(x, c, conv_bufs, s0s, lbs, w_ada, b_ada, w_in, hgrn_gnorm, conv_w, w_a_out, w_b_out, w_o, ln_g, ln_b, state_dtype):
    hgrn_states = []
    conv_states = []
    for l in range(DEPTH):
        x, buf_new, s_new = trunk_layer(x, c, conv_bufs[l], s0s[l], lbs[l], w_ada[l], b_ada[l], w_in[l], hgrn_gnorm[l],
                                        conv_w[l], w_a_out[l], w_b_out[l], w_o[l], ln_g[l], ln_b[l])
        hgrn_states.append(s_new.astype(state_dtype))
        conv_states.append(buf_new.astype(state_dtype))
    return x, jnp.stack(hgrn_states), jnp.stack(conv_states)


def setup_inputs(seed: int = 0) -> dict:
    key = jax.random.key(seed)
    ks = jax.random.split(key, 20)
    f32 = jnp.float32
    x_prompt = jax.random.normal(ks[0], (BATCH, SEQ, D_MODEL), f32)
    x_sample = jax.random.normal(ks[1], (DEC_BATCH, DEC_SEQ, D_MODEL), f32)
    state_hgrn = 0.5 * jax.random.normal(ks[2], (DEPTH, DEC_BATCH, A_HEADS, A_HEAD_DIM, A_HEAD_DIM), f32)
    state_conv = jax.random.normal(ks[3], (DEPTH, DEC_BATCH, CONV_W - 1, B_WIDTH), f32)
    c_prompt = jax.random.normal(ks[4], (BATCH, D_MODEL), f32)
    c_sample = jax.random.normal(ks[5], (DEC_BATCH, D_MODEL), f32)
    w_ada = jax.random.normal(ks[6], (DEPTH, D_MODEL, 3 * D_MODEL), f32) * D_MODEL ** -0.5
    b_ada = 0.02 * jax.random.normal(ks[7], (DEPTH, 3 * D_MODEL), f32)
    w_in = jax.random.normal(ks[8], (DEPTH, D_MODEL, N_PROJ), f32) * D_MODEL ** -0.5
    hgrn_lb = 0.1 * jax.random.normal(ks[9], (DEPTH, A_WIDTH), f32)
    hgrn_gnorm = 1.0 + 0.01 * jax.random.normal(ks[10], (DEPTH, A_WIDTH), f32)
    conv_w = jax.random.normal(ks[11], (DEPTH, CONV_W, B_WIDTH), f32) * CONV_W ** -0.5
    w_a_out = jax.random.normal(ks[12], (DEPTH, A_WIDTH, D_MODEL), f32) * (A_WIDTH ** -0.5) * DEEPNORM_BETA
    w_b_out = jax.random.normal(ks[13], (DEPTH, B_WIDTH, D_MODEL), f32) * (B_WIDTH ** -0.5) * DEEPNORM_BETA
    w_o = jax.random.normal(ks[14], (DEPTH, D_MODEL, D_MODEL), f32) * (D_MODEL ** -0.5) * DEEPNORM_BETA
    ln_g = 1.0 + 0.01 * jax.random.normal(ks[15], (DEPTH, D_MODEL), f32)
    ln_b = 0.01 * jax.random.normal(ks[16], (DEPTH, D_MODEL), f32)
    return {"x_prompt": x_prompt, "x_sample": x_sample, "state_hgrn": state_hgrn, "state_conv": state_conv,
            "c_prompt": c_prompt, "c_sample": c_sample, "w_ada": w_ada, "b_ada": b_ada, "w_in": w_in,
            "hgrn_lb": hgrn_lb, "hgrn_gnorm": hgrn_gnorm, "conv_w": conv_w, "w_a_out": w_a_out,
            "w_b_out": w_b_out, "w_o": w_o, "ln_g": ln_g, "ln_b": ln_b}


def reference(x_prompt, x_sample, state_hgrn, state_conv, c_prompt, c_sample, w_ada, b_ada, w_in, hgrn_lb,
              hgrn_gnorm, conv_w, w_a_out, w_b_out, w_o, ln_g, ln_b):
    lbs = hgrn2_lower_bounds(hgrn_lb)
    sd = state_hgrn.dtype
    zero_hgrn = jnp.zeros((DEPTH, BATCH, A_HEADS, A_HEAD_DIM, A_HEAD_DIM), jnp.float32)
    zero_conv = jnp.zeros((DEPTH, BATCH, CONV_W - 1, B_WIDTH), x_prompt.dtype)
    y_prompt, new_hgrn_prompt, new_conv_prompt = run_trunk(
        x_prompt, c_prompt, zero_conv, zero_hgrn, lbs, w_ada, b_ada, w_in, hgrn_gnorm, conv_w,
        w_a_out, w_b_out, w_o, ln_g, ln_b, sd)
    y_sample, new_hgrn_sample, new_conv_sample = run_trunk(
        x_sample, c_sample, state_conv, state_hgrn, lbs, w_ada, b_ada, w_in, hgrn_gnorm, conv_w,
        w_a_out, w_b_out, w_o, ln_g, ln_b, sd)
    return (y_prompt, y_sample, new_hgrn_prompt, new_conv_prompt, new_hgrn_sample, new_conv_sample)
```

```python
import functools

import jax
import jax.numpy as jnp
from jax import lax
from jax.experimental import pallas as pl
from jax.experimental.pallas import tpu as pltpu

F32 = jnp.float32
BF16 = jnp.bfloat16

D_MODEL = 1024
N_HEADS = 8
HEAD_DIM = 128
N_GROUPS = 10
LANE_TILE = 256
N_TILES = D_MODEL // LANE_TILE
CONV_W = 3
CHUNK = 64
DEPTH = 4
DEEPNORM_ALPHA = (2 * DEPTH) ** 0.25
LN_EPS = 1e-5
RMS_EPS = 1e-6
LB_FLOOR = 1e-30
Q_SCALE = HEAD_DIM ** -0.5
PROMPT_LEVELS = (1, 2, 4, 8, 16, 32)
SAMPLE_LEVELS = (1, 2)
SAMPLE_T = 4
SAMPLE_SEQ_PER_STEP = 4
SAMPLE_GROUP_SEQ = 32
VMEM_LIMIT_BYTES = 58 * 1024 * 1024

_NT = (((1,), (1,)), ((), ()))
_TN = (((0,), (0,)), ((), ()))


def _sigmoid(x):
    return 1.0 / (1.0 + jnp.exp(-x))


def _roll_rows(x, shift):
    return pltpu.roll(x, shift % x.shape[0], axis=0)


def _mid_row_broadcast(p, h, row):
    n, w = p.shape
    b = 2 * h
    if b >= 8:
        blocks = p.reshape(n // b, b, w)
        mid = blocks[:, h - 1:h, :]
        return jnp.broadcast_to(mid, (n // b, b, w)).reshape(n, w)
    pos = row & (b - 1)
    out = p
    for src in range(b):
        if src == h - 1:
            continue
        out = jnp.where(pos == src, _roll_rows(p, src - (h - 1)), out)
    return out


def _last_row_broadcast(p, b, row):
    pos = row & (b - 1)
    out = p
    for src in range(b - 1):
        out = jnp.where(pos == src, _roll_rows(p, src - (b - 1)), out)
    return out


def _tree_factors(q, k, lf, levels, row):
    p = lf
    xs = []
    for h in levels:
        mid = _mid_row_broadcast(p, h, row)
        second = (row & h) != 0
        e = jnp.where(second, p, mid - p)
        xs.append((jnp.where(second, q, k) * jnp.exp(e)).astype(BF16))
        p = p + jnp.where(second, mid, 0.0)
    return xs, p


def _intra_scores(xs, levels, qk, lanes, tt, ss):
    xr = tt ^ ss
    lower = ss < tt
    a = jnp.where(tt == ss, jnp.sum(qk[:, lanes], axis=1, keepdims=True), 0.0)
    for x, h in zip(xs, levels):
        xh = x[:, lanes]
        g = lax.dot_general(xh, xh, _NT, preferred_element_type=F32)
        a = a + jnp.where(lower & (xr >= h) & (xr < 2 * h), g, 0.0)
    return a


def _project_tile(j, hb, win_ref, lb_ref, cw_ref, prev_u1, prev_u2, fix_rows,
                  q_s, k_s, lf_s, v_s, ga_s, yb_s, ra_s, rb_s, u_s):
    def proj(g):
        return jnp.dot(hb, win_ref[j, g], preferred_element_type=F32)

    qv = proj(0)
    q_s[j] = qv * _sigmoid(qv) * Q_SCALE
    lb = lb_ref[j]
    one_m = 1.0 - lb
    sig = _sigmoid(proj(1))
    lf_s[j] = jnp.log(jnp.maximum(lb, LB_FLOOR) + one_m * sig)
    k_s[j] = one_m * (1.0 - sig)
    v_s[j] = proj(2).astype(BF16)
    ga = proj(3)
    ga_s[j] = ga * _sigmoid(ga)
    bg = proj(4)
    u = proj(5) * proj(6)
    gb = proj(7)
    cw = cw_ref[j]
    u1 = jnp.where(fix_rows[0], prev_u1, _roll_rows(u, 1))
    u2 = jnp.where(fix_rows[1], prev_u2, _roll_rows(u, 2))
    conv = u2 * cw[0:1, :] + u1 * cw[1:2, :] + u * cw[2:3, :]
    yb_s[j] = (bg * conv * (gb * _sigmoid(gb))).astype(BF16)
    u_s[j] = u
    ra_s[j] = _sigmoid(proj(8))
    rb_s[j] = _sigmoid(proj(9))


def _output_phase(x, gate, o_s, ga_s, yb_s, ra_s, rb_s, gn_ref, wa_ref, wb_ref, wo_ref, lng_ref, lnb_ref):
    ya = None
    yb = None
    for j in range(N_TILES):
        o = o_s[j]
        parts = []
        for hh in range(LANE_TILE // HEAD_DIM):
            oh = o[:, hh * HEAD_DIM:(hh + 1) * HEAD_DIM]
            ms = jnp.mean(oh * oh, axis=1, keepdims=True)
            parts.append(oh * lax.rsqrt(ms + RMS_EPS))
        on = jnp.concatenate(parts, axis=1) * gn_ref[j]
        da = jnp.dot((on * ga_s[j]).astype(BF16), wa_ref[j], preferred_element_type=F32)
        db = jnp.dot(yb_s[j], wb_ref[j], preferred_element_type=F32)
        ya = da if ya is None else ya + da
        yb = db if yb is None else yb + db
    out = None
    for j in range(N_TILES):
        lanes = slice(j * LANE_TILE, (j + 1) * LANE_TILE)
        m = ra_s[j] * ya[:, lanes] + rb_s[j] * yb[:, lanes]
        dm = jnp.dot(m.astype(BF16), wo_ref[j], preferred_element_type=F32)
        out = dm if out is None else out + dm
    res = DEEPNORM_ALPHA * x + gate * out
    mu = jnp.mean(res, axis=1, keepdims=True)
    cen = res - mu
    var = jnp.mean(cen * cen, axis=1, keepdims=True)
    return cen * lax.rsqrt(var + LN_EPS) * lng_ref[...] + lnb_ref[...]


def _modulated_input(x, ada):
    shift = ada[:, 0:D_MODEL]
    scale = ada[:, D_MODEL:2 * D_MODEL]
    gate = ada[:, 2 * D_MODEL:3 * D_MODEL]
    return x * (1.0 + scale) + shift, gate


def _prompt_layer_kernel(x_ref, ada_ref, lb_ref, gn_ref, cw_ref, lng_ref, lnb_ref,
                         win_ref, wa_ref, wb_ref, wo_ref,
                         y_ref, sfin_ref, cfin_ref,
                         hb_s, q_s, k_s, lf_s, v_s, ga_s, yb_s, ra_s, rb_s, u_s, o_s, st_s, cprev_s):
    t_blk = pl.program_id(1)
    tb = x_ref.shape[1]

    @pl.when(t_blk == 0)
    def _():
        st_s[...] = jnp.zeros_like(st_s)
        cprev_s[...] = jnp.zeros_like(cprev_s)

    x = x_ref[0]
    h, gate = _modulated_input(x, ada_ref[0])
    hb_s[...] = h.astype(BF16)

    rows_tb = lax.broadcasted_iota(jnp.int32, (tb, 1), 0)
    fix_rows = (rows_tb == 0, rows_tb < 2)

    def project(j, carry):
        prev = cprev_s[j]
        prev_u1 = prev[7:8, :]
        prev_u2 = jnp.where(rows_tb == 0, prev[6:7, :], prev[7:8, :])
        _project_tile(j, hb_s[...], win_ref, lb_ref, cw_ref, prev_u1, prev_u2, fix_rows,
                      q_s, k_s, lf_s, v_s, ga_s, yb_s, ra_s, rb_s, u_s)
        cprev_s[j] = u_s[j, tb - 8:tb, :]
        return carry

    lax.fori_loop(0, N_TILES, project, 0)

    row = lax.broadcasted_iota(jnp.int32, (CHUNK, 1), 0)
    tt = lax.broadcasted_iota(jnp.int32, (CHUNK, CHUNK), 0)
    ss = lax.broadcasted_iota(jnp.int32, (CHUNK, CHUNK), 1)

    def recur(idx, carry):
        c = idx // N_TILES
        j = idx % N_TILES
        rows = pl.ds(pl.multiple_of(c * CHUNK, CHUNK), CHUNK)
        q = q_s[j, rows, :]
        k = k_s[j, rows, :]
        lf = lf_s[j, rows, :]
        v = v_s[j, rows, :]
        xs, gc = _tree_factors(q, k, lf, PROMPT_LEVELS, row)
        glast = gc[CHUNK - 1:CHUNK, :]
        dec = jnp.exp(glast)
        qg = (q * jnp.exp(gc)).astype(BF16)
        kd = (k * jnp.exp(glast - gc)).astype(BF16)
        qk = q * k
        outs = []
        for hh in range(LANE_TILE // HEAD_DIM):
            lanes = slice(hh * HEAD_DIM, (hh + 1) * HEAD_DIM)
            hd = j * (LANE_TILE // HEAD_DIM) + hh
            a = _intra_scores(xs, PROMPT_LEVELS, qk, lanes, tt, ss)
            st = st_s[hd]
            o = lax.dot_general(qg[:, lanes], st.astype(BF16), _NT, preferred_element_type=F32)
            o = o + jnp.dot(a.astype(BF16), v[:, lanes], preferred_element_type=F32)
            outs.append(o)
            st_s[hd] = st * dec[:, lanes] + lax.dot_general(v[:, lanes], kd[:, lanes], _TN,
                                                             preferred_element_type=F32)
        o_s[j, rows, :] = jnp.concatenate(outs, axis=1)
        return carry

    lax.fori_loop(0, (tb // CHUNK) * N_TILES, recur, 0)

    y_ref[0] = _output_phase(x, gate, o_s, ga_s, yb_s, ra_s, rb_s, gn_ref, wa_ref, wb_ref, wo_ref,
                             lng_ref, lnb_ref)

    @pl.when(t_blk == pl.num_programs(1) - 1)
    def _():
        for hd in range(N_HEADS):
            sfin_ref[0, hd] = st_s[hd].T
        for j in range(N_TILES):
            cfin_ref[0, :, j * LANE_TILE:(j + 1) * LANE_TILE] = cprev_s[j, 8 - (CONV_W - 1):8, :]


def _const_spec(shape):
    nd = len(shape)
    return pl.BlockSpec(shape, lambda *_: (0,) * nd, pipeline_mode=pl.Buffered(1))


def _tile_scratch(rows, with_u=True):
    f = lambda dt: pltpu.VMEM((N_TILES, rows, LANE_TILE), dt)
    return ([f(F32), f(F32), f(F32), f(BF16), f(F32), f(BF16), f(F32), f(F32)]
            + ([f(F32)] if with_u else []) + [f(F32)])


def _prompt_layer(x, ada, lb, gn, cw, lng, lnb, win, wa, wb, wo, *, block_t):
    bsz, seq, _ = x.shape
    grid = (bsz, seq // block_t)
    out_shape = (
        jax.ShapeDtypeStruct((bsz, seq, D_MODEL), F32),
        jax.ShapeDtypeStruct((bsz, N_HEADS, HEAD_DIM, HEAD_DIM), F32),
        jax.ShapeDtypeStruct((bsz, CONV_W - 1, D_MODEL), F32),
    )
    in_specs = [
        pl.BlockSpec((1, block_t, D_MODEL), lambda b, t: (b, t, 0)),
        pl.BlockSpec((1, 1, 3 * D_MODEL), lambda b, t: (b, 0, 0)),
        _const_spec(lb.shape), _const_spec(gn.shape), _const_spec(cw.shape),
        _const_spec(lng.shape), _const_spec(lnb.shape),
        _const_spec(win.shape), _const_spec(wa.shape), _const_spec(wb.shape), _const_spec(wo.shape),
    ]
    out_specs = (
        pl.BlockSpec((1, block_t, D_MODEL), lambda b, t: (b, t, 0)),
        pl.BlockSpec((1, N_HEADS, HEAD_DIM, HEAD_DIM), lambda b, t: (b, 0, 0, 0)),
        pl.BlockSpec((1, CONV_W - 1, D_MODEL), lambda b, t: (b, 0, 0)),
    )
    scratch = ([pltpu.VMEM((block_t, D_MODEL), BF16)] + _tile_scratch(block_t)
               + [pltpu.VMEM((N_HEADS, HEAD_DIM, HEAD_DIM), F32),
                  pltpu.VMEM((N_TILES, 8, LANE_TILE), F32)])
    return pl.pallas_call(
        _prompt_layer_kernel,
        grid=grid,
        in_specs=in_specs,
        out_specs=out_specs,
        out_shape=out_shape,
        scratch_shapes=scratch,
        compiler_params=pltpu.CompilerParams(
            dimension_semantics=("arbitrary", "arbitrary"),
            vmem_limit_bytes=VMEM_LIMIT_BYTES),
        name="prompt_layer",
    )(x, ada, lb, gn, cw, lng, lnb, win, wa, wb, wo)


def _sample_layer_kernel(x_ref, ada_ref, cpad_ref, lb_ref, gn_ref, cw_ref, lng_ref, lnb_ref,
                         win_ref, wa_ref, wb_ref, wo_ref, sin_ref,
                         y_ref, sout_ref, u_ref,
                         hb_s, q_s, k_s, lf_s, v_s, ga_s, yb_s, ra_s, rb_s, o_s, qg_s, kdt_s, dct_s):
    step = pl.program_id(1)
    n_rows = x_ref.shape[0]

    @pl.when(step == 0)
    def _():
        h, _ = _modulated_input(x_ref[...], ada_ref[...])
        hb_s[...] = h.astype(BF16)
        rows_all = lax.broadcasted_iota(jnp.int32, (n_rows, 1), 0)
        tpos = rows_all & (SAMPLE_T - 1)
        fix_rows = (tpos == 0, tpos < 2)

        def project(j, carry):
            cpad = cpad_ref[j]
            _project_tile(j, hb_s[...], win_ref, lb_ref, cw_ref, _roll_rows(cpad, -1), cpad, fix_rows,
                          q_s, k_s, lf_s, v_s, ga_s, yb_s, ra_s, rb_s, u_ref)
            return carry

        lax.fori_loop(0, N_TILES, project, 0)

        row = lax.broadcasted_iota(jnp.int32, (CHUNK, 1), 0)
        tt = lax.broadcasted_iota(jnp.int32, (CHUNK, CHUNK), 0)
        ss = lax.broadcasted_iota(jnp.int32, (CHUNK, CHUNK), 1)

        def intra(idx, carry):
            c = idx // N_TILES
            j = idx % N_TILES
            rows = pl.ds(pl.multiple_of(c * CHUNK, CHUNK), CHUNK)
            q = q_s[j, rows, :]
            k = k_s[j, rows, :]
            v = v_s[j, rows, :]
            xs, gc = _tree_factors(q, k, lf_s[j, rows, :], SAMPLE_LEVELS, row)
            glast = _last_row_broadcast(gc, SAMPLE_T, row)
            qg_s[j, rows, :] = q * jnp.exp(gc)
            k_s[j, rows, :] = k * jnp.exp(glast - gc)
            lf_s[j, rows, :] = jnp.exp(glast)
            qk = q * k
            outs = []
            for hh in range(LANE_TILE // HEAD_DIM):
                lanes = slice(hh * HEAD_DIM, (hh + 1) * HEAD_DIM)
                a = _intra_scores(xs, SAMPLE_LEVELS, qk, lanes, tt, ss)
                outs.append(jnp.dot(a.astype(BF16), v[:, lanes], preferred_element_type=F32))
            o_s[j, rows, :] = jnp.concatenate(outs, axis=1)
            return carry

        lax.fori_loop(0, (n_rows // CHUNK) * N_TILES, intra, 0)

        for hd in range(N_HEADS):
            j, hh = divmod(hd, LANE_TILE // HEAD_DIM)
            lanes = slice(hh * HEAD_DIM, (hh + 1) * HEAD_DIM)
            kdt_s[hd] = k_s[j, :, lanes].T.astype(BF16)
            dct_s[hd] = lf_s[j, :, lanes].T

    tile_rows = SAMPLE_SEQ_PER_STEP * SAMPLE_T
    r0 = pl.multiple_of(step * tile_rows, tile_rows)
    lane_id = lax.broadcasted_iota(jnp.int32, (1, n_rows), 1)
    row_id = lax.broadcasted_iota(jnp.int32, (tile_rows, 1), 0)

    def seq_state(j, carry):
        for hh in range(LANE_TILE // HEAD_DIM):
            lanes = slice(hh * HEAD_DIM, (hh + 1) * HEAD_DIM)
            hd = j * (LANE_TILE // HEAD_DIM) + hh
            qg = qg_s[j, pl.ds(r0, tile_rows), lanes].astype(BF16)
            vblk = v_s[j, :, lanes]
            kdt = kdt_s[hd]
            dct = dct_s[hd]
            o_acc = jnp.zeros((tile_rows, HEAD_DIM), F32)
            for s in range(SAMPLE_SEQ_PER_STEP):
                st = sin_ref[s, hd]
                res = jnp.dot(qg, st.astype(BF16), preferred_element_type=F32)
                in_seq = (row_id >= s * SAMPLE_T) & (row_id < (s + 1) * SAMPLE_T)
                o_acc = o_acc + jnp.where(in_seq, res, 0.0)
                c0 = r0 + s * SAMPLE_T
                tok = (lane_id >= c0) & (lane_id < c0 + SAMPLE_T)
                upd = jnp.dot(jnp.where(tok, kdt, jnp.zeros_like(kdt)), vblk, preferred_element_type=F32)
                dcol = jnp.sum(jnp.where(lane_id == c0, dct, 0.0), axis=1, keepdims=True)
                sout_ref[s, hd] = st * dcol + upd
            o_s[j, pl.ds(r0, tile_rows), lanes] = o_s[j, pl.ds(r0, tile_rows), lanes] + o_acc
        return carry

    lax.fori_loop(0, N_TILES, seq_state, 0)

    @pl.when(step == pl.num_programs(1) - 1)
    def _():
        _, gate = _modulated_input(x_ref[...], ada_ref[...])
        y_ref[...] = _output_phase(x_ref[...], gate, o_s, ga_s, yb_s, ra_s, rb_s, gn_ref,
                                   wa_ref, wb_ref, wo_ref, lng_ref, lnb_ref)


def _sample_layer(x, ada, cpad, lb, gn, cw, lng, lnb, win, wa, wb, wo, state):
    n_rows = x.shape[0]
    n_seq = state.shape[0]
    grp_rows = SAMPLE_GROUP_SEQ * SAMPLE_T
    steps = SAMPLE_GROUP_SEQ // SAMPLE_SEQ_PER_STEP
    grid = (n_seq // SAMPLE_GROUP_SEQ, steps)
    state_blk = (SAMPLE_SEQ_PER_STEP, N_HEADS, HEAD_DIM, HEAD_DIM)
    out_shape = (
        jax.ShapeDtypeStruct((n_rows, D_MODEL), F32),
        jax.ShapeDtypeStruct(state.shape, F32),
        jax.ShapeDtypeStruct((N_TILES, n_rows, LANE_TILE), F32),
    )
    in_specs = [
        pl.BlockSpec((grp_rows, D_MODEL), lambda g, s: (g, 0)),
        pl.BlockSpec((grp_rows, 3 * D_MODEL), lambda g, s: (g, 0)),
        pl.BlockSpec((N_TILES, grp_rows, LANE_TILE), lambda g, s: (0, g, 0)),
        _const_spec(lb.shape), _const_spec(gn.shape), _const_spec(cw.shape),
        _const_spec(lng.shape), _const_spec(lnb.shape),
        _const_spec(win.shape), _const_spec(wa.shape), _const_spec(wb.shape), _const_spec(wo.shape),
        pl.BlockSpec(state_blk, lambda g, s: (g * steps + s, 0, 0, 0)),
    ]
    out_specs = (
        pl.BlockSpec((grp_rows, D_MODEL), lambda g, s: (g, 0)),
        pl.BlockSpec(state_blk, lambda g, s: (g * steps + s, 0, 0, 0)),
        pl.BlockSpec((N_TILES, grp_rows, LANE_TILE), lambda g, s: (0, g, 0)),
    )
    scratch = ([pltpu.VMEM((grp_rows, D_MODEL), BF16)] + _tile_scratch(grp_rows, with_u=False)
               + [pltpu.VMEM((N_TILES, grp_rows, LANE_TILE), F32),
                  pltpu.VMEM((N_HEADS, HEAD_DIM, grp_rows), BF16),
                  pltpu.VMEM((N_HEADS, HEAD_DIM, grp_rows), F32)])
    return pl.pallas_call(
        _sample_layer_kernel,
        grid=grid,
        in_specs=in_specs,
        out_specs=out_specs,
        out_shape=out_shape,
        scratch_shapes=scratch,
        compiler_params=pltpu.CompilerParams(
            dimension_semantics=("arbitrary", "arbitrary"),
            vmem_limit_bytes=VMEM_LIMIT_BYTES),
        name="sample_layer",
    )(x, ada, cpad, lb, gn, cw, lng, lnb, win, wa, wb, wo, state)


def _prep_kernel(c_ref, wada_ref, bada_ref, lbp_ref, ada_ref, lbs_ref):
    c = c_ref[...]
    sc = (c * _sigmoid(c)).astype(BF16)
    ada_ref[0] = jnp.dot(sc, wada_ref[0], preferred_element_type=F32) + bada_ref[0]
    p = [lbp_ref[i:i + 1, :] for i in range(DEPTH)]
    mx = functools.reduce(jnp.maximum, p)
    e = [jnp.exp(pi - mx) for pi in p]
    tot = functools.reduce(lambda a, b: a + b, e)
    acc = jnp.zeros_like(mx)
    lbs_ref[0:1, :] = acc
    for i in range(1, DEPTH):
        acc = acc + e[i] / tot
        lbs_ref[i:i + 1, :] = acc


def _prep(c_all, w_ada, b_ada, lb_param):
    n = c_all.shape[0]
    return pl.pallas_call(
        _prep_kernel,
        grid=(DEPTH,),
        in_specs=[
            pl.BlockSpec((n, D_MODEL), lambda l: (0, 0)),
            pl.BlockSpec((1, D_MODEL, 3 * D_MODEL), lambda l: (l, 0, 0)),
            pl.BlockSpec((1, 1, 3 * D_MODEL), lambda l: (l, 0, 0)),
            pl.BlockSpec((DEPTH, D_MODEL), lambda l: (0, 0)),
        ],
        out_specs=(
            pl.BlockSpec((1, n, 3 * D_MODEL), lambda l: (l, 0, 0)),
            pl.BlockSpec((DEPTH, D_MODEL), lambda l: (0, 0)),
        ),
        out_shape=(
            jax.ShapeDtypeStruct((DEPTH, n, 3 * D_MODEL), F32),
            jax.ShapeDtypeStruct((DEPTH, D_MODEL), F32),
        ),
        compiler_params=pltpu.CompilerParams(dimension_semantics=("arbitrary",)),
        name="conditioning",
    )(c_all, w_ada, b_ada.reshape(DEPTH, 1, 3 * D_MODEL), lb_param)


def _tile_rows(p):
    r = p.shape[0]
    return p.reshape(r, N_TILES, LANE_TILE).transpose(1, 0, 2)


def kernel(x_prompt, x_sample, state_hgrn, state_conv, c_prompt, c_sample, w_ada, b_ada, w_in, hgrn_lb,
           hgrn_gnorm, conv_w, w_a_out, w_b_out, w_o, ln_g, ln_b):
    n_prompt = x_prompt.shape[0]
    n_seq, t_dec, _ = x_sample.shape
    assert t_dec == SAMPLE_T and x_prompt.shape[1] % CHUNK == 0
    block_t = min(256, x_prompt.shape[1])

    c_all = jnp.concatenate([c_prompt, c_sample], axis=0)
    ada_all, lbs = _prep(c_all, w_ada.astype(BF16), b_ada, hgrn_lb)

    win_t = (w_in.astype(BF16).reshape(DEPTH, D_MODEL, N_GROUPS, N_TILES, LANE_TILE)
             .transpose(0, 3, 2, 1, 4))
    wa_t = w_a_out.astype(BF16).reshape(DEPTH, N_TILES, LANE_TILE, D_MODEL)
    wb_t = w_b_out.astype(BF16).reshape(DEPTH, N_TILES, LANE_TILE, D_MODEL)
    wo_t = w_o.astype(BF16).reshape(DEPTH, N_TILES, LANE_TILE, D_MODEL)

    xp = x_prompt
    xs = x_sample.reshape(n_seq * t_dec, D_MODEL)
    hgrn_p, conv_p, hgrn_s, conv_s = [], [], [], []
    for l in range(DEPTH):
        lb = _tile_rows(lbs[l:l + 1])
        gn = _tile_rows(hgrn_gnorm[l:l + 1])
        cw = _tile_rows(conv_w[l])
        lng = ln_g[l:l + 1]
        lnb = ln_b[l:l + 1]
        ada_p = ada_all[l, :n_prompt].reshape(n_prompt, 1, 3 * D_MODEL)
        xp, s_new, c_new = _prompt_layer(xp, ada_p, lb, gn, cw, lng, lnb,
                                         win_t[l], wa_t[l], wb_t[l], wo_t[l], block_t=block_t)
        hgrn_p.append(s_new)
        conv_p.append(c_new)

        ada_s = jnp.repeat(ada_all[l, n_prompt:], t_dec, axis=0)
        cpad = jnp.pad(state_conv[l], ((0, 0), (0, t_dec - (CONV_W - 1)), (0, 0)))
        cpad = _tile_rows(cpad.reshape(n_seq * t_dec, D_MODEL))
        xs, s_new, u_all = _sample_layer(xs, ada_s, cpad, lb, gn, cw, lng, lnb,
                                         win_t[l], wa_t[l], wb_t[l], wo_t[l], state_hgrn[l])
        hgrn_s.append(s_new)
        u_rows = u_all.transpose(1, 0, 2).reshape(n_seq, t_dec, D_MODEL)
        conv_s.append(u_rows[:, t_dec - (CONV_W - 1):, :])

    return (xp, xs.reshape(n_seq, t_dec, D_MODEL), jnp.stack(hgrn_p), jnp.stack(conv_p),
            jnp.stack(hgrn_s), jnp.stack(conv_s))
```

```python
import functools

import jax
import jax.numpy as jnp
from jax import lax
from jax.experimental import pallas as pl
from jax.experimental.pallas import tpu as pltpu

F32 = jnp.float32
BF16 = jnp.bfloat16

D_MODEL = 1024
N_HEADS = 8
HEAD_DIM = 128
N_GROUPS = 10
LANE_TILE = 256
N_TILES = D_MODEL // LANE_TILE
CONV_W = 3
CHUNK = 64
DEPTH = 4
DEEPNORM_ALPHA = (2 * DEPTH) ** 0.25
LN_EPS = 1e-5
RMS_EPS = 1e-6
LB_FLOOR = 1e-30
Q_SCALE = HEAD_DIM ** -0.5
LOG2_E = 1.4426950408889634
PROMPT_LEVELS = (1, 2, 4, 8, 16, 32)
SAMPLE_LEVELS = (1, 2)
SAMPLE_T = 4
SAMPLE_SEQ_PER_STEP = 4
SAMPLE_GROUP_SEQ = 32
VMEM_LIMIT_BYTES = 58 * 1024 * 1024

_NT = (((1,), (1,)), ((), ()))
_TN = (((0,), (0,)), ((), ()))


def _sigmoid(x):
    return 1.0 / (1.0 + jnp.exp2(x * (-LOG2_E)))


def _roll_rows(x, shift):
    return pltpu.roll(x, shift % x.shape[0], axis=0)


def _roll_in_groups(x, shift):
    n, w = x.shape
    return pltpu.roll(x.reshape(n // 8, 8, w), shift % 8, axis=1).reshape(n, w)


def _halves(x, h):
    n = x.shape[0]
    first = [x[b:b + h] for b in range(0, n, 2 * h)]
    second = [x[b + h:b + 2 * h] for b in range(0, n, 2 * h)]
    return first, second


def _interleave(first, second):
    out = []
    for a, b in zip(first, second):
        out += [a, b]
    return jnp.concatenate(out, axis=0)


def _tree_factors(q, k, lf, levels, row):
    p = lf
    tot = lf
    xs = []
    for h in levels:
        if h < 8:
            second = (row & h) != 0
            other = _roll_in_groups(tot, h)
            if h != 4:
                other = jnp.where(second, other, _roll_in_groups(tot, -h))
            x = jnp.where(second, q, k) * jnp.exp2(jnp.where(second, p, tot - p))
            p = p + jnp.where(second, other, 0.0)
            tot = tot + other
        else:
            p1, p2 = _halves(p, h)
            t1, t2 = _halves(tot, h)
            k1, _ = _halves(k, h)
            _, q2 = _halves(q, h)
            x = _interleave([kk * jnp.exp2(tt - pp) for kk, tt, pp in zip(k1, t1, p1)],
                            [qq * jnp.exp2(pp) for qq, pp in zip(q2, p2)])
            p = _interleave(p1, [pp + tt for pp, tt in zip(p2, t1)])
            both = [a + b for a, b in zip(t1, t2)]
            tot = _interleave(both, both)
        xs.append(x.astype(BF16))
    return xs, p, tot


def _pair_level_code(n, levels):
    t = lax.broadcasted_iota(jnp.int32, (n, 2 * n), 0)
    s = lax.broadcasted_iota(jnp.int32, (n, 2 * n), 1) & (n - 1)
    x = t ^ s
    code = jnp.where(s == t, 0, -1)
    for i, h in enumerate(levels):
        code = jnp.where((s < t) & (x >= h) & (x < 2 * h), i + 1, code)
    return code


def _block_diag(x):
    w = x.shape[1] // 2
    z = jnp.zeros((x.shape[0], w), x.dtype)
    return jnp.concatenate([jnp.concatenate([x[:, :w], z], axis=1),
                            jnp.concatenate([z, x[:, w:]], axis=1)], axis=0)


def _pair_scores(xs, levels, qk, code):
    n = qk.shape[0]
    w = qk.shape[1] // 2
    lane = lax.broadcasted_iota(jnp.int32, (1, 2 * n), 1)
    diag = jnp.where(lane < n, jnp.sum(qk[:, :w], axis=1, keepdims=True),
                     jnp.sum(qk[:, w:], axis=1, keepdims=True))
    grams = [lax.dot_general(x, _block_diag(x), _NT, preferred_element_type=F32) for x in xs]
    pieces = []
    for r0 in range(0, n, 8):
        rows = slice(r0, r0 + 8)
        c = code[rows]
        a = jnp.where(c == 0, diag[rows], 0.0)
        for i, h in enumerate(levels):
            if h < 8 or (r0 & h):
                a = jnp.where(c == i + 1, grams[i][rows], a)
        pieces.append(a)
    return jnp.concatenate(pieces, axis=0)


def _project_tile(j, hb, win_ref, lb_ref, cw_ref, prev_u1, prev_u2, fix_rows,
                  q_s, k_s, lf_s, v_s, ga_s, yb_s, ra_s, rb_s, u_s):
    def proj(g):
        col = g * D_MODEL + j * LANE_TILE
        return jnp.dot(hb, win_ref[:, col:col + LANE_TILE], preferred_element_type=F32)

    qv = proj(0)
    q_s[j] = qv * _sigmoid(qv) * Q_SCALE
    lb = lb_ref[j]
    one_m = 1.0 - lb
    sig = _sigmoid(proj(1))
    lf_s[j] = jnp.log(jnp.maximum(lb, LB_FLOOR) + one_m * sig) * LOG2_E
    k_s[j] = one_m * (1.0 - sig)
    v_s[j] = proj(2).astype(BF16)
    ga = proj(3)
    ga_s[j] = ga * _sigmoid(ga)
    bg = proj(4)
    u = proj(5) * proj(6)
    gb = proj(7)
    cw = cw_ref[j]
    u1 = jnp.where(fix_rows[0], prev_u1, _roll_rows(u, 1))
    u2 = jnp.where(fix_rows[1], prev_u2, _roll_rows(u, 2))
    conv = u2 * cw[0:1, :] + u1 * cw[1:2, :] + u * cw[2:3, :]
    yb_s[j] = (bg * conv * (gb * _sigmoid(gb))).astype(BF16)
    u_s[j] = u
    ra_s[j] = _sigmoid(proj(8))
    rb_s[j] = _sigmoid(proj(9))


def _output_phase(x, gate, o_s, ga_s, yb_s, ra_s, rb_s, gn_ref, wa_ref, wb_ref, wo_ref, lng_ref, lnb_ref):
    ya = None
    yb = None
    for j in range(N_TILES):
        o = o_s[j]
        parts = []
        for hh in range(LANE_TILE // HEAD_DIM):
            oh = o[:, hh * HEAD_DIM:(hh + 1) * HEAD_DIM]
            ms = jnp.mean(oh * oh, axis=1, keepdims=True)
            parts.append(oh * lax.rsqrt(ms + RMS_EPS))
        on = jnp.concatenate(parts, axis=1) * gn_ref[j]
        da = jnp.dot((on * ga_s[j]).astype(BF16), wa_ref[j], preferred_element_type=F32)
        db = jnp.dot(yb_s[j], wb_ref[j], preferred_element_type=F32)
        ya = da if ya is None else ya + da
        yb = db if yb is None else yb + db
    out = None
    for j in range(N_TILES):
        lanes = slice(j * LANE_TILE, (j + 1) * LANE_TILE)
        m = ra_s[j] * ya[:, lanes] + rb_s[j] * yb[:, lanes]
        dm = jnp.dot(m.astype(BF16), wo_ref[j], preferred_element_type=F32)
        out = dm if out is None else out + dm
    res = DEEPNORM_ALPHA * x + gate * out
    mu = jnp.mean(res, axis=1, keepdims=True)
    cen = res - mu
    var = jnp.mean(cen * cen, axis=1, keepdims=True)
    return cen * lax.rsqrt(var + LN_EPS) * lng_ref[...] + lnb_ref[...]


def _modulated_input(x, ada):
    shift = ada[:, 0:D_MODEL]
    scale = ada[:, D_MODEL:2 * D_MODEL]
    gate = ada[:, 2 * D_MODEL:3 * D_MODEL]
    return x * (1.0 + scale) + shift, gate


def _prompt_layer_kernel(x_ref, ada_ref, lb_ref, gn_ref, cw_ref, lng_ref, lnb_ref,
                         win_ref, wa_ref, wb_ref, wo_ref,
                         y_ref, sfin_ref, cfin_ref,
                         hb_s, q_s, k_s, lf_s, v_s, ga_s, yb_s, ra_s, rb_s, u_s, o_s, st_s, cprev_s):
    t_blk = pl.program_id(1)
    tb = x_ref.shape[1]

    @pl.when(t_blk == 0)
    def _():
        st_s[...] = jnp.zeros_like(st_s)
        cprev_s[...] = jnp.zeros_like(cprev_s)

    x = x_ref[0]
    h, gate = _modulated_input(x, ada_ref[0])
    hb_s[...] = h.astype(BF16)

    rows_tb = lax.broadcasted_iota(jnp.int32, (tb, 1), 0)
    fix_rows = (rows_tb == 0, rows_tb < 2)

    def project(j, carry):
        prev = cprev_s[j]
        prev_u1 = prev[7:8, :]
        prev_u2 = jnp.where(rows_tb == 0, prev[6:7, :], prev[7:8, :])
        _project_tile(j, hb_s[...], win_ref, lb_ref, cw_ref, prev_u1, prev_u2, fix_rows,
                      q_s, k_s, lf_s, v_s, ga_s, yb_s, ra_s, rb_s, u_s)
        cprev_s[j] = u_s[j, tb - 8:tb, :]
        return carry

    for j in range(N_TILES):
        project(j, 0)

    row = lax.broadcasted_iota(jnp.int32, (CHUNK, 1), 0)
    code = _pair_level_code(CHUNK, PROMPT_LEVELS)

    def recur(c, carry):
        rows = pl.ds(c * CHUNK, CHUNK)
        for j in range(N_TILES):
            q = q_s[j, rows, :]
            k = k_s[j, rows, :]
            v = v_s[j, rows, :]
            xs, gc, tot = _tree_factors(q, k, lf_s[j, rows, :], PROMPT_LEVELS, row)
            dec = jnp.exp2(tot[0:1, :])
            qg = (q * jnp.exp2(gc)).astype(BF16)
            kd = (k * jnp.exp2(tot - gc)).astype(BF16)
            a = _pair_scores(xs, PROMPT_LEVELS, q * k, code)
            o = jnp.dot(a.astype(BF16), _block_diag(v), preferred_element_type=F32)
            inter = []
            for hh in range(LANE_TILE // HEAD_DIM):
                lanes = slice(hh * HEAD_DIM, (hh + 1) * HEAD_DIM)
                hd = j * (LANE_TILE // HEAD_DIM) + hh
                st = st_s[hd]
                inter.append(lax.dot_general(qg[:, lanes], st.astype(BF16), _NT,
                                             preferred_element_type=F32))
                st_s[hd] = st * dec[:, lanes] + lax.dot_general(v[:, lanes], kd[:, lanes], _TN,
                                                                 preferred_element_type=F32)
            o_s[j, rows, :] = o + jnp.concatenate(inter, axis=1)
        return carry

    for c in range(tb // CHUNK):
        recur(c, 0)

    y_ref[0] = _output_phase(x, gate, o_s, ga_s, yb_s, ra_s, rb_s, gn_ref, wa_ref, wb_ref, wo_ref,
                             lng_ref, lnb_ref)

    @pl.when(t_blk == pl.num_programs(1) - 1)
    def _():
        for hd in range(N_HEADS):
            sfin_ref[0, hd] = st_s[hd].T
        for j in range(N_TILES):
            cfin_ref[0, :, j * LANE_TILE:(j + 1) * LANE_TILE] = cprev_s[j, 8 - (CONV_W - 1):8, :]


def _const_spec(shape):
    nd = len(shape)
    return pl.BlockSpec(shape, lambda *_: (0,) * nd, pipeline_mode=pl.Buffered(1))


def _tile_scratch(rows, with_u=True):
    f = lambda dt: pltpu.VMEM((N_TILES, rows, LANE_TILE), dt)
    return ([f(F32), f(F32), f(F32), f(BF16), f(F32), f(BF16), f(F32), f(F32)]
            + ([f(F32)] if with_u else []) + [f(F32)])


def _prompt_layer(x, ada, lb, gn, cw, lng, lnb, win, wa, wb, wo, *, block_t):
    bsz, seq, _ = x.shape
    grid = (bsz, seq // block_t)
    out_shape = (
        jax.ShapeDtypeStruct((bsz, seq, D_MODEL), F32),
        jax.ShapeDtypeStruct((bsz, N_HEADS, HEAD_DIM, HEAD_DIM), F32),
        jax.ShapeDtypeStruct((bsz, CONV_W - 1, D_MODEL), F32),
    )
    in_specs = [
        pl.BlockSpec((1, block_t, D_MODEL), lambda b, t: (b, t, 0)),
        pl.BlockSpec((1, 1, 3 * D_MODEL), lambda b, t: (b, 0, 0)),
        _const_spec(lb.shape), _const_spec(gn.shape), _const_spec(cw.shape),
        _const_spec(lng.shape), _const_spec(lnb.shape),
        _const_spec(win.shape), _const_spec(wa.shape), _const_spec(wb.shape), _const_spec(wo.shape),
    ]
    out_specs = (
        pl.BlockSpec((1, block_t, D_MODEL), lambda b, t: (b, t, 0)),
        pl.BlockSpec((1, N_HEADS, HEAD_DIM, HEAD_DIM), lambda b, t: (b, 0, 0, 0)),
        pl.BlockSpec((1, CONV_W - 1, D_MODEL), lambda b, t: (b, 0, 0)),
    )
    scratch = ([pltpu.VMEM((block_t, D_MODEL), BF16)] + _tile_scratch(block_t)
               + [pltpu.VMEM((N_HEADS, HEAD_DIM, HEAD_DIM), F32),
                  pltpu.VMEM((N_TILES, 8, LANE_TILE), F32)])
    return pl.pallas_call(
        _prompt_layer_kernel,
        grid=grid,
        in_specs=in_specs,
        out_specs=out_specs,
        out_shape=out_shape,
        scratch_shapes=scratch,
        compiler_params=pltpu.CompilerParams(
            dimension_semantics=("arbitrary", "arbitrary"),
            vmem_limit_bytes=VMEM_LIMIT_BYTES),
        name="prompt_layer",
    )(x, ada, lb, gn, cw, lng, lnb, win, wa, wb, wo)


def _sample_layer_kernel(x_ref, ada_ref, cpad_ref, lb_ref, gn_ref, cw_ref, lng_ref, lnb_ref,
                         win_ref, wa_ref, wb_ref, wo_ref, sin_ref,
                         y_ref, sout_ref, u_ref,
                         hb_s, q_s, k_s, lf_s, v_s, ga_s, yb_s, ra_s, rb_s, o_s, qg_s, kdt_s, dct_s):
    step = pl.program_id(1)
    n_rows = x_ref.shape[0]

    @pl.when(step == 0)
    def _():
        h, _ = _modulated_input(x_ref[...], ada_ref[...])
        hb_s[...] = h.astype(BF16)
        rows_all = lax.broadcasted_iota(jnp.int32, (n_rows, 1), 0)
        tpos = rows_all & (SAMPLE_T - 1)
        fix_rows = (tpos == 0, tpos < 2)

        for j in range(N_TILES):
            cpad = cpad_ref[j]
            _project_tile(j, hb_s[...], win_ref, lb_ref, cw_ref, _roll_rows(cpad, -1), cpad, fix_rows,
                          q_s, k_s, lf_s, v_s, ga_s, yb_s, ra_s, rb_s, u_ref)

        row = lax.broadcasted_iota(jnp.int32, (CHUNK, 1), 0)
        code = _pair_level_code(CHUNK, SAMPLE_LEVELS)

        def intra(idx, carry):
            c = idx // N_TILES
            j = idx % N_TILES
            rows = pl.ds(pl.multiple_of(c * CHUNK, CHUNK), CHUNK)
            q = q_s[j, rows, :]
            k = k_s[j, rows, :]
            xs, gc, tot = _tree_factors(q, k, lf_s[j, rows, :], SAMPLE_LEVELS, row)
            qg_s[j, rows, :] = q * jnp.exp2(gc)
            k_s[j, rows, :] = k * jnp.exp2(tot - gc)
            lf_s[j, rows, :] = jnp.exp2(tot)
            a = _pair_scores(xs, SAMPLE_LEVELS, q * k, code)
            o_s[j, rows, :] = jnp.dot(a.astype(BF16), _block_diag(v_s[j, rows, :]),
                                      preferred_element_type=F32)
            return carry

        lax.fori_loop(0, (n_rows // CHUNK) * N_TILES, intra, 0)

        for hd in range(N_HEADS):
            j, hh = divmod(hd, LANE_TILE // HEAD_DIM)
            lanes = slice(hh * HEAD_DIM, (hh + 1) * HEAD_DIM)
            kdt_s[hd] = k_s[j, :, lanes].T.astype(BF16)
            dct_s[hd] = lf_s[j, :, lanes].T

    tile_rows = SAMPLE_SEQ_PER_STEP * SAMPLE_T
    r0 = pl.multiple_of(step * tile_rows, tile_rows)
    lane_id = lax.broadcasted_iota(jnp.int32, (1, n_rows), 1)
    row_id = lax.broadcasted_iota(jnp.int32, (tile_rows, 1), 0)

    def seq_state(j, carry):
        for hh in range(LANE_TILE // HEAD_DIM):
            lanes = slice(hh * HEAD_DIM, (hh + 1) * HEAD_DIM)
            hd = j * (LANE_TILE // HEAD_DIM) + hh
            qg = qg_s[j, pl.ds(r0, tile_rows), lanes].astype(BF16)
            vblk = v_s[j, :, lanes]
            kdt = kdt_s[hd]
            dct = dct_s[hd]
            o_acc = jnp.zeros((tile_rows, HEAD_DIM), F32)
            for s in range(SAMPLE_SEQ_PER_STEP):
                st = sin_ref[s, hd]
                res = jnp.dot(qg, st.astype(BF16), preferred_element_type=F32)
                in_seq = (row_id >= s * SAMPLE_T) & (row_id < (s + 1) * SAMPLE_T)
                o_acc = o_acc + jnp.where(in_seq, res, 0.0)
                c0 = r0 + s * SAMPLE_T
                tok = (lane_id >= c0) & (lane_id < c0 + SAMPLE_T)
                upd = jnp.dot(jnp.where(tok, kdt, jnp.zeros_like(kdt)), vblk, preferred_element_type=F32)
                dcol = jnp.sum(jnp.where(lane_id == c0, dct, 0.0), axis=1, keepdims=True)
                sout_ref[s, hd] = st * dcol + upd
            o_s[j, pl.ds(r0, tile_rows), lanes] = o_s[j, pl.ds(r0, tile_rows), lanes] + o_acc
        return carry

    lax.fori_loop(0, N_TILES, seq_state, 0)

    @pl.when(step == pl.num_programs(1) - 1)
    def _():
        _, gate = _modulated_input(x_ref[...], ada_ref[...])
        y_ref[...] = _output_phase(x_ref[...], gate, o_s, ga_s, yb_s, ra_s, rb_s, gn_ref,
                                   wa_ref, wb_ref, wo_ref, lng_ref, lnb_ref)


def _sample_layer(x, ada, cpad, lb, gn, cw, lng, lnb, win, wa, wb, wo, state):
    n_rows = x.shape[0]
    n_seq = state.shape[0]
    grp_rows = SAMPLE_GROUP_SEQ * SAMPLE_T
    steps = SAMPLE_GROUP_SEQ // SAMPLE_SEQ_PER_STEP
    grid = (n_seq // SAMPLE_GROUP_SEQ, steps)
    state_blk = (SAMPLE_SEQ_PER_STEP, N_HEADS, HEAD_DIM, HEAD_DIM)
    out_shape = (
        jax.ShapeDtypeStruct((n_rows, D_MODEL), F32),
        jax.ShapeDtypeStruct(state.shape, F32),
        jax.ShapeDtypeStruct((N_TILES, n_rows, LANE_TILE), F32),
    )
    in_specs = [
        pl.BlockSpec((grp_rows, D_MODEL), lambda g, s: (g, 0)),
        pl.BlockSpec((grp_rows, 3 * D_MODEL), lambda g, s: (g, 0)),
        pl.BlockSpec((N_TILES, grp_rows, LANE_TILE), lambda g, s: (0, g, 0)),
        _const_spec(lb.shape), _const_spec(gn.shape), _const_spec(cw.shape),
        _const_spec(lng.shape), _const_spec(lnb.shape),
        _const_spec(win.shape), _const_spec(wa.shape), _const_spec(wb.shape), _const_spec(wo.shape),
        pl.BlockSpec(state_blk, lambda g, s: (g * steps + s, 0, 0, 0)),
    ]
    out_specs = (
        pl.BlockSpec((grp_rows, D_MODEL), lambda g, s: (g, 0)),
        pl.BlockSpec(state_blk, lambda g, s: (g * steps + s, 0, 0, 0)),
        pl.BlockSpec((N_TILES, grp_rows, LANE_TILE), lambda g, s: (0, g, 0)),
    )
    scratch = ([pltpu.VMEM((grp_rows, D_MODEL), BF16)] + _tile_scratch(grp_rows, with_u=False)
               + [pltpu.VMEM((N_TILES, grp_rows, LANE_TILE), F32),
                  pltpu.VMEM((N_HEADS, HEAD_DIM, grp_rows), BF16),
                  pltpu.VMEM((N_HEADS, HEAD_DIM, grp_rows), F32)])
    return pl.pallas_call(
        _sample_layer_kernel,
        grid=grid,
        in_specs=in_specs,
        out_specs=out_specs,
        out_shape=out_shape,
        scratch_shapes=scratch,
        compiler_params=pltpu.CompilerParams(
            dimension_semantics=("arbitrary", "arbitrary"),
            vmem_limit_bytes=VMEM_LIMIT_BYTES),
        name="sample_layer",
    )(x, ada, cpad, lb, gn, cw, lng, lnb, win, wa, wb, wo, state)


def _prep_kernel(c_ref, wada_ref, bada_ref, lbp_ref, ada_ref, lbs_ref):
    c = c_ref[...]
    sc = (c * _sigmoid(c)).astype(BF16)
    ada_ref[0] = jnp.dot(sc, wada_ref[0], preferred_element_type=F32) + bada_ref[0]
    p = [lbp_ref[i:i + 1, :] for i in range(DEPTH)]
    mx = functools.reduce(jnp.maximum, p)
    e = [jnp.exp(pi - mx) for pi in p]
    tot = functools.reduce(lambda a, b: a + b, e)
    acc = jnp.zeros_like(mx)
    lbs_ref[0:1, :] = acc
    for i in range(1, DEPTH):
        acc = acc + e[i] / tot
        lbs_ref[i:i + 1, :] = acc


def _prep(c_all, w_ada, b_ada, lb_param):
    n = c_all.shape[0]
    return pl.pallas_call(
        _prep_kernel,
        grid=(DEPTH,),
        in_specs=[
            pl.BlockSpec((n, D_MODEL), lambda l: (0, 0)),
            pl.BlockSpec((1, D_MODEL, 3 * D_MODEL), lambda l: (l, 0, 0)),
            pl.BlockSpec((1, 1, 3 * D_MODEL), lambda l: (l, 0, 0)),
            pl.BlockSpec((DEPTH, D_MODEL), lambda l: (0, 0)),
        ],
        out_specs=(
            pl.BlockSpec((1, n, 3 * D_MODEL), lambda l: (l, 0, 0)),
            pl.BlockSpec((DEPTH, D_MODEL), lambda l: (0, 0)),
        ),
        out_shape=(
            jax.ShapeDtypeStruct((DEPTH, n, 3 * D_MODEL), F32),
            jax.ShapeDtypeStruct((DEPTH, D_MODEL), F32),
        ),
        compiler_params=pltpu.CompilerParams(dimension_semantics=("arbitrary",)),
        name="conditioning",
    )(c_all, w_ada, b_ada.reshape(DEPTH, 1, 3 * D_MODEL), lb_param)


def _tile_rows(p):
    r = p.shape[0]
    return p.reshape(r, N_TILES, LANE_TILE).transpose(1, 0, 2)


def kernel(x_prompt, x_sample, state_hgrn, state_conv, c_prompt, c_sample, w_ada, b_ada, w_in, hgrn_lb,
           hgrn_gnorm, conv_w, w_a_out, w_b_out, w_o, ln_g, ln_b):
    n_prompt = x_prompt.shape[0]
    n_seq, t_dec, _ = x_sample.shape
    assert t_dec == SAMPLE_T and x_prompt.shape[1] % CHUNK == 0
    block_t = min(256, x_prompt.shape[1])

    c_all = jnp.concatenate([c_prompt, c_sample], axis=0)
    ada_all, lbs = _prep(c_all, w_ada.astype(BF16), b_ada, hgrn_lb)

    win_t = w_in.astype(BF16)
    wa_t = w_a_out.astype(BF16).reshape(DEPTH, N_TILES, LANE_TILE, D_MODEL)
    wb_t = w_b_out.astype(BF16).reshape(DEPTH, N_TILES, LANE_TILE, D_MODEL)
    wo_t = w_o.astype(BF16).reshape(DEPTH, N_TILES, LANE_TILE, D_MODEL)

    xp = x_prompt
    xs = x_sample.reshape(n_seq * t_dec, D_MODEL)
    hgrn_p, conv_p, hgrn_s, conv_s = [], [], [], []
    for l in range(DEPTH):
        lb = _tile_rows(lbs[l:l + 1])
        gn = _tile_rows(hgrn_gnorm[l:l + 1])
        cw = _tile_rows(conv_w[l])
        lng = ln_g[l:l + 1]
        lnb = ln_b[l:l + 1]
        ada_p = ada_all[l, :n_prompt].reshape(n_prompt, 1, 3 * D_MODEL)
        xp, s_new, c_new = _prompt_layer(xp, ada_p, lb, gn, cw, lng, lnb,
                                         win_t[l], wa_t[l], wb_t[l], wo_t[l], block_t=block_t)
        hgrn_p.append(s_new)
        conv_p.append(c_new)

        ada_s = jnp.repeat(ada_all[l, n_prompt:], t_dec, axis=0)
        cpad = jnp.pad(state_conv[l], ((0, 0), (0, t_dec - (CONV_W - 1)), (0, 0)))
        cpad = _tile_rows(cpad.reshape(n_seq * t_dec, D_MODEL))
        xs, s_new, u_all = _sample_layer(xs, ada_s, cpad, lb, gn, cw, lng, lnb,
                                         win_t[l], wa_t[l], wb_t[l], wo_t[l], state_hgrn[l])
        hgrn_s.append(s_new)
        u_rows = u_all.transpose(1, 0, 2).reshape(n_seq, t_dec, D_MODEL)
        conv_s.append(u_rows[:, t_dec - (CONV_W - 1):, :])

    return (xp, xs.reshape(n_seq, t_dec, D_MODEL), jnp.stack(hgrn_p), jnp.stack(conv_p),
            jnp.stack(hgrn_s), jnp.stack(conv_s))
```

```python
import functools
import itertools

import jax
import jax.numpy as jnp
from jax import lax
from jax.experimental import pallas as pl
from jax.experimental.pallas import tpu as pltpu

F32 = jnp.float32
BF16 = jnp.bfloat16

D_MODEL = 1024
N_HEADS = 8
HEAD_DIM = 128
N_GROUPS = 10
LANE_TILE = 256
HEADS_PER_TILE = LANE_TILE // HEAD_DIM
N_TILES = D_MODEL // LANE_TILE
CONV_W = 3
CHUNK = 64
DEPTH = 4
DEEPNORM_ALPHA = (2 * DEPTH) ** 0.25
LN_EPS = 1e-5
RMS_EPS = 1e-6
LB_FLOOR = 1e-30
Q_SCALE = HEAD_DIM ** -0.5
LOG2_E = 1.4426950408889634
TREE_LEVELS = (1, 2, 4, 8, 16, 32)
PROMPT_BLOCK_T = 256
SAMPLE_T = 4
SAMPLE_GROUP_SEQ = 32
SAMPLE_SEQ_PER_STEP = 4
VMEM_LIMIT_BYTES = 58 * 1024 * 1024

_NT = (((1,), (1,)), ((), ()))
_TN = (((0,), (0,)), ((), ()))


def _sigmoid(x):
    return 1.0 / (1.0 + jnp.exp2(x * (-LOG2_E)))


def _roll_rows(x, shift):
    return pltpu.roll(x, shift % x.shape[0], axis=0)


def _roll_in_groups(x, shift):
    n, w = x.shape
    return pltpu.roll(x.reshape(n // 8, 8, w), shift % 8, axis=1).reshape(n, w)


def _halves(x, h):
    n = x.shape[0]
    first = [x[b:b + h] for b in range(0, n, 2 * h)]
    second = [x[b + h:b + 2 * h] for b in range(0, n, 2 * h)]
    return first, second


def _interleave(first, second):
    out = []
    for a, b in zip(first, second):
        out += [a, b]
    return jnp.concatenate(out, axis=0)


def _tree_factors(q, k, lf, levels, row):
    p = lf
    tot = lf
    xs = []
    for h in levels:
        if h < 8:
            second = (row & h) != 0
            other = _roll_in_groups(tot, h)
            if h != 4:
                other = jnp.where(second, other, _roll_in_groups(tot, -h))
            x = jnp.where(second, q, k) * jnp.exp2(jnp.where(second, p, tot - p))
            p = p + jnp.where(second, other, 0.0)
            tot = tot + other
        else:
            p1, p2 = _halves(p, h)
            t1, t2 = _halves(tot, h)
            k1, _ = _halves(k, h)
            _, q2 = _halves(q, h)
            x = _interleave([kk * jnp.exp2(tt - pp) for kk, tt, pp in zip(k1, t1, p1)],
                            [qq * jnp.exp2(pp) for qq, pp in zip(q2, p2)])
            p = _interleave(p1, [pp + tt for pp, tt in zip(p2, t1)])
            both = [a + b for a, b in zip(t1, t2)]
            tot = _interleave(both, both)
        xs.append(x.astype(BF16))
    return xs, p, tot


def _pair_level_code(n, levels):
    t = lax.broadcasted_iota(jnp.int32, (n, 2 * n), 0)
    s = lax.broadcasted_iota(jnp.int32, (n, 2 * n), 1) & (n - 1)
    x = t ^ s
    code = jnp.where(s == t, 0, -1)
    for i, h in enumerate(levels):
        code = jnp.where((s < t) & (x >= h) & (x < 2 * h), i + 1, code)
    return code


def _block_diag(x):
    w = x.shape[1] // 2
    z = jnp.zeros((x.shape[0], w), x.dtype)
    return jnp.concatenate([jnp.concatenate([x[:, :w], z], axis=1),
                            jnp.concatenate([z, x[:, w:]], axis=1)], axis=0)


def _block_diag_t(x):
    n = x.shape[0]
    w = x.shape[1] // 2
    z = jnp.zeros((w - n, w), x.dtype)
    t0 = jnp.concatenate([x[:, :w], z], axis=0).T
    t1 = jnp.concatenate([z, x[:, w:]], axis=0).T
    return jnp.concatenate([t0, t1], axis=0)


def _pair_scores(xs, levels, qk, code, fill=None):
    n = qk.shape[0]
    w = qk.shape[1] // 2
    lane = lax.broadcasted_iota(jnp.int32, (1, 2 * n), 1)
    diag = jnp.where(lane < n, jnp.sum(qk[:, :w], axis=1, keepdims=True),
                     jnp.sum(qk[:, w:], axis=1, keepdims=True))
    grams = [jnp.dot(x, _block_diag_t(x), preferred_element_type=F32) for x in xs]
    if fill is not None:
        fill()
    pieces = []
    for r0 in range(0, n, 8):
        rows = slice(r0, r0 + 8)
        c = code[rows]
        a = jnp.where(c == 0, diag[rows], 0.0)
        for i, h in enumerate(levels):
            if h < 8 or (r0 & h):
                a = jnp.where(c == i + 1, grams[i][rows], a)
        pieces.append(a)
    return jnp.concatenate(pieces, axis=0)


def _proj(hb, win_ref, g, j):
    col = g * D_MODEL + j * LANE_TILE
    return jnp.dot(hb, win_ref[:, col:col + LANE_TILE], preferred_element_type=F32)


def _project_recurrent(j, hb, win_ref, lb_ref, q_s, k_s, lf_s, v_s):
    qv = _proj(hb, win_ref, 0, j)
    q_s[j] = qv * _sigmoid(qv) * Q_SCALE
    lb = lb_ref[j]
    one_m = 1.0 - lb
    sig = _sigmoid(_proj(hb, win_ref, 1, j))
    lf_s[j] = jnp.log(jnp.maximum(lb, LB_FLOOR) + one_m * sig) * LOG2_E
    k_s[j] = one_m * (1.0 - sig)
    v_s[j] = _proj(hb, win_ref, 2, j).astype(BF16)


def _project_gates(j, hb, win_ref, cw_ref, delayed, ga_s, yb_s, ra_s, rb_s, u_s):
    proj = functools.partial(_proj, hb, win_ref, j=j)
    ga = proj(3)
    ga_s[j] = ga * _sigmoid(ga)
    yield
    u = proj(5)
    yield
    u = u * proj(6)
    cw = cw_ref[j]
    u1, u2 = delayed(u)
    conv = u2 * cw[0:1, :] + u1 * cw[1:2, :] + u * cw[2:3, :]
    u_s[j] = u
    yield
    conv = conv * proj(4)
    yield
    gb = proj(7)
    yb_s[j] = (conv * (gb * _sigmoid(gb))).astype(BF16)
    yield
    ra_s[j] = _sigmoid(proj(8))
    yield
    rb_s[j] = _sigmoid(proj(9))
    yield


def _output_phase(x, gate, o_s, ga_s, yb_s, ra_s, rb_s, gn_ref, wa_ref, wb_ref, wo_ref, lng_ref, lnb_ref):
    ya = None
    yb = None
    for j in range(N_TILES):
        o = o_s[j]
        parts = []
        for hh in range(HEADS_PER_TILE):
            oh = o[:, hh * HEAD_DIM:(hh + 1) * HEAD_DIM]
            ms = jnp.mean(oh * oh, axis=1, keepdims=True)
            parts.append(oh * lax.rsqrt(ms + RMS_EPS))
        on = jnp.concatenate(parts, axis=1) * gn_ref[j]
        da = jnp.dot((on * ga_s[j]).astype(BF16), wa_ref[j], preferred_element_type=F32)
        db = jnp.dot(yb_s[j], wb_ref[j], preferred_element_type=F32)
        ya = da if ya is None else ya + da
        yb = db if yb is None else yb + db
    out = None
    for j in range(N_TILES):
        lanes = slice(j * LANE_TILE, (j + 1) * LANE_TILE)
        m = ra_s[j] * ya[:, lanes] + rb_s[j] * yb[:, lanes]
        dm = jnp.dot(m.astype(BF16), wo_ref[j], preferred_element_type=F32)
        out = dm if out is None else out + dm
    res = DEEPNORM_ALPHA * x + gate * out
    mu = jnp.mean(res, axis=1, keepdims=True)
    cen = res - mu
    var = jnp.mean(cen * cen, axis=1, keepdims=True)
    return cen * lax.rsqrt(var + LN_EPS) * lng_ref[...] + lnb_ref[...]


def _modulated_input(x, ada):
    shift = ada[:, 0:D_MODEL]
    scale = ada[:, D_MODEL:2 * D_MODEL]
    gate = ada[:, 2 * D_MODEL:3 * D_MODEL]
    return x * (1.0 + scale) + shift, gate


def _prompt_layer_kernel(x_ref, ada_ref, lb_ref, gn_ref, cw_ref, lng_ref, lnb_ref,
                         win_ref, wa_ref, wb_ref, wo_ref,
                         y_ref, sfin_ref, cfin_ref,
                         hb_s, q_s, k_s, lf_s, v_s, ga_s, yb_s, ra_s, rb_s, u_s, o_s, st_s, cprev_s):
    t_blk = pl.program_id(1)
    tb = x_ref.shape[1]

    @pl.when(t_blk == 0)
    def _():
        st_s[...] = jnp.zeros_like(st_s)
        cprev_s[...] = jnp.zeros_like(cprev_s)

    x = x_ref[0]
    h, gate = _modulated_input(x, ada_ref[0])
    hb_s[...] = h.astype(BF16)

    rows_tb = lax.broadcasted_iota(jnp.int32, (tb, 1), 0)

    for j in range(N_TILES):
        _project_recurrent(j, hb_s[...], win_ref.at[0], lb_ref.at[0], q_s, k_s, lf_s, v_s)

    row = lax.broadcasted_iota(jnp.int32, (CHUNK, 1), 0)
    code = _pair_level_code(CHUNK, TREE_LEVELS)

    def gates(j):
        prev = cprev_s[j]

        def delayed(u):
            u1 = jnp.where(rows_tb == 0, prev[7:8, :], _roll_rows(u, 1))
            u2 = jnp.where(rows_tb == 0, prev[6:7, :],
                           jnp.where(rows_tb == 1, prev[7:8, :], _roll_rows(u, 2)))
            return u1, u2

        yield from _project_gates(j, hb_s[...], win_ref.at[0], cw_ref.at[0], delayed,
                                  ga_s, yb_s, ra_s, rb_s, u_s)
        cprev_s[j] = u_s[j, tb - 8:tb, :]

    filler = itertools.chain.from_iterable(gates(j) for j in range(N_TILES))
    fill = lambda: next(filler, None)

    for c in range(tb // CHUNK):
        rows = pl.ds(c * CHUNK, CHUNK)
        for j in range(N_TILES):
            q = q_s[j, rows, :]
            k = k_s[j, rows, :]
            v = v_s[j, rows, :]
            xs, gc, tot = _tree_factors(q, k, lf_s[j, rows, :], TREE_LEVELS, row)
            dec = jnp.exp2(tot[0:1, :])
            qg = (q * jnp.exp2(gc)).astype(BF16)
            kd = (k * jnp.exp2(tot - gc)).astype(BF16)
            a = _pair_scores(xs, TREE_LEVELS, q * k, code, fill)
            o = jnp.dot(a.astype(BF16), _block_diag(v), preferred_element_type=F32)
            inter = []
            for hh in range(HEADS_PER_TILE):
                lanes = slice(hh * HEAD_DIM, (hh + 1) * HEAD_DIM)
                hd = j * HEADS_PER_TILE + hh
                st = st_s[hd]
                inter.append(lax.dot_general(qg[:, lanes], st.astype(BF16), _NT,
                                             preferred_element_type=F32))
                st_s[hd] = st * dec[:, lanes] + lax.dot_general(v[:, lanes], kd[:, lanes], _TN,
                                                                 preferred_element_type=F32)
            fill()
            o_s[j, rows, :] = o + jnp.concatenate(inter, axis=1)
    for _ in filler:
        pass

    y_ref[0] = _output_phase(x, gate, o_s, ga_s, yb_s, ra_s, rb_s, gn_ref.at[0], wa_ref.at[0],
                             wb_ref.at[0], wo_ref.at[0], lng_ref.at[0], lnb_ref.at[0])

    @pl.when(t_blk == pl.num_programs(1) - 1)
    def _():
        for hd in range(N_HEADS):
            sfin_ref[0, hd] = st_s[hd].T
        for j in range(N_TILES):
            cfin_ref[0, :, j * LANE_TILE:(j + 1) * LANE_TILE] = cprev_s[j, 8 - (CONV_W - 1):8, :]


def _layer_spec(arr, index_of_layer):
    nd = arr.ndim
    return pl.BlockSpec((1,) + arr.shape[1:], lambda *ids: (index_of_layer(*ids),) + (0,) * (nd - 1),
                        pipeline_mode=pl.Buffered(1))


def _tile_scratch(rows):
    f = lambda dt: pltpu.VMEM((N_TILES, rows, LANE_TILE), dt)
    return [f(F32), f(F32), f(F32), f(BF16), f(F32), f(BF16), f(F32), f(F32), f(F32), f(F32)]


def _prompt_layer(layer, x, ada, params, weights):
    bsz, seq, _ = x.shape
    block_t = min(PROMPT_BLOCK_T, seq)
    grid = (bsz, seq // block_t)
    out_shape = (
        jax.ShapeDtypeStruct((bsz, seq, D_MODEL), F32),
        jax.ShapeDtypeStruct((bsz, N_HEADS, HEAD_DIM, HEAD_DIM), F32),
        jax.ShapeDtypeStruct((bsz, CONV_W - 1, D_MODEL), F32),
    )
    at_layer = lambda b, t: layer
    in_specs = ([pl.BlockSpec((1, block_t, D_MODEL), lambda b, t: (b, t, 0)),
                 pl.BlockSpec((1, 1, 3 * D_MODEL), lambda b, t: (b, 0, 0))]
                + [_layer_spec(p, at_layer) for p in params]
                + [_layer_spec(w, at_layer) for w in weights])
    out_specs = (
        pl.BlockSpec((1, block_t, D_MODEL), lambda b, t: (b, t, 0)),
        pl.BlockSpec((1, N_HEADS, HEAD_DIM, HEAD_DIM), lambda b, t: (b, 0, 0, 0)),
        pl.BlockSpec((1, CONV_W - 1, D_MODEL), lambda b, t: (b, 0, 0)),
    )
    scratch = ([pltpu.VMEM((block_t, D_MODEL), BF16)] + _tile_scratch(block_t)
               + [pltpu.VMEM((N_HEADS, HEAD_DIM, HEAD_DIM), F32),
                  pltpu.VMEM((N_TILES, 8, LANE_TILE), F32)])
    return pl.pallas_call(
        _prompt_layer_kernel,
        grid=grid,
        in_specs=in_specs,
        out_specs=out_specs,
        out_shape=out_shape,
        scratch_shapes=scratch,
        compiler_params=pltpu.CompilerParams(
            dimension_semantics=("arbitrary", "arbitrary"),
            vmem_limit_bytes=VMEM_LIMIT_BYTES),
        name="prompt_layer",
    )(x, ada, *params, *weights)


def _sample_kernel(x_ref, ada_ref, cin_ref, lb_ref, gn_ref, cw_ref, lng_ref, lnb_ref,
                   win_ref, wa_ref, wb_ref, wo_ref, sin_ref,
                   y_ref, sout_ref, cout_ref,
                   xcur_s, hb_s, q_s, k_s, lf_s, v_s, ga_s, yb_s, ra_s, rb_s, u_s, o_s, kdt_s, dct_s):
    layer = pl.program_id(0)
    grp = pl.program_id(1)
    step = pl.program_id(2)
    nseq = SAMPLE_GROUP_SEQ
    n_rows = nseq * SAMPLE_T
    grp_rows = pl.ds(pl.multiple_of(grp * n_rows, n_rows), n_rows)
    tok = lambda t: slice(t * nseq, (t + 1) * nseq)

    @pl.when(step == 0)
    def _():
        @pl.when(layer == 0)
        def _():
            xcur_s[grp_rows, :] = x_ref[...]

        ada = ada_ref[0]
        x = xcur_s[grp_rows, :]
        for t in range(SAMPLE_T):
            h, _ = _modulated_input(x[tok(t)], ada)
            hb_s[tok(t), :] = h.astype(BF16)

        for j in range(N_TILES):
            lanes_j = slice(j * LANE_TILE, (j + 1) * LANE_TILE)
            hist0 = cin_ref[0, 0, :, lanes_j]
            hist1 = cin_ref[0, 1, :, lanes_j]

            def delayed(u):
                return (jnp.concatenate([hist1, u[:(SAMPLE_T - 1) * nseq]], axis=0),
                        jnp.concatenate([hist0, hist1, u[:(SAMPLE_T - 2) * nseq]], axis=0))

            _project_recurrent(j, hb_s[...], win_ref.at[0], lb_ref.at[0], q_s, k_s, lf_s, v_s)
            for _ in _project_gates(j, hb_s[...], win_ref.at[0], cw_ref.at[0], delayed,
                                    ga_s, yb_s, ra_s, rb_s, u_s):
                pass

            q = [q_s[j, tok(t), :] for t in range(SAMPLE_T)]
            k = [k_s[j, tok(t), :] for t in range(SAMPLE_T)]
            v = [v_s[j, tok(t), :].astype(F32) for t in range(SAMPLE_T)]
            gc = [lf_s[j, tok(0), :]]
            for t in range(1, SAMPLE_T):
                gc.append(gc[-1] + lf_s[j, tok(t), :])
            tot = gc[-1]
            for t in range(SAMPLE_T):
                o_heads = [jnp.zeros((nseq, HEAD_DIM), F32) for _ in range(HEADS_PER_TILE)]
                for s in range(t + 1):
                    w = q[t] * k[s]
                    if s < t:
                        w = w * jnp.exp2(gc[t] - gc[s])
                    for hh in range(HEADS_PER_TILE):
                        lanes = slice(hh * HEAD_DIM, (hh + 1) * HEAD_DIM)
                        a = jnp.sum(w[:, lanes], axis=1, keepdims=True)
                        o_heads[hh] = o_heads[hh] + a * v[s][:, lanes]
                o_s[j, tok(t), :] = jnp.concatenate(o_heads, axis=1)
            for t in range(SAMPLE_T):
                q_s[j, tok(t), :] = q[t] * jnp.exp2(gc[t])
                k_s[j, tok(t), :] = k[t] * jnp.exp2(tot - gc[t])
                lf_s[j, tok(t), :] = jnp.exp2(tot)
            for hh in range(HEADS_PER_TILE):
                lanes = slice(hh * HEAD_DIM, (hh + 1) * HEAD_DIM)
                hd = j * HEADS_PER_TILE + hh
                kdt_s[hd] = k_s[j, :, lanes].T.astype(BF16)
                dct_s[hd] = lf_s[j, :, lanes].T

    seq0 = step * SAMPLE_SEQ_PER_STEP
    piece0 = pl.multiple_of((seq0 // 8) * 8, 8)
    lane_id = lax.broadcasted_iota(jnp.int32, (1, n_rows), 1)
    sub_id = lax.broadcasted_iota(jnp.int32, (8 * SAMPLE_T, 1), 0) & 7

    def seq_state(j, carry):
        for hh in range(HEADS_PER_TILE):
            lanes = slice(hh * HEAD_DIM, (hh + 1) * HEAD_DIM)
            hd = j * HEADS_PER_TILE + hh
            qg = jnp.concatenate([q_s[j, pl.ds(t * nseq + piece0, 8), lanes] for t in range(SAMPLE_T)],
                                 axis=0).astype(BF16)
            vblk = v_s[j, :, lanes]
            kdt = kdt_s[hd]
            dct = dct_s[hd]
            o_acc = jnp.zeros((8 * SAMPLE_T, HEAD_DIM), F32)
            for s in range(SAMPLE_SEQ_PER_STEP):
                seq = seq0 + s
                st = sin_ref[0, s, hd]
                res = jnp.dot(qg, st.astype(BF16), preferred_element_type=F32)
                o_acc = o_acc + jnp.where(sub_id == (seq & 7), res, 0.0)
                mine = (lane_id & (nseq - 1)) == seq
                upd = jnp.dot(jnp.where(mine, kdt, jnp.zeros_like(kdt)), vblk, preferred_element_type=F32)
                dcol = jnp.sum(jnp.where(lane_id == seq, dct, 0.0), axis=1, keepdims=True)
                sout_ref[0, s, hd] = st * dcol + upd
            for t in range(SAMPLE_T):
                rows = pl.ds(t * nseq + piece0, 8)
                o_s[j, rows, lanes] = o_s[j, rows, lanes] + o_acc[8 * t:8 * (t + 1)]
        return carry

    lax.fori_loop(0, N_TILES, seq_state, 0)

    @pl.when(step == pl.num_programs(2) - 1)
    def _():
        ada = ada_ref[0]
        gate = jnp.concatenate([ada[:, 2 * D_MODEL:3 * D_MODEL]] * SAMPLE_T, axis=0)
        y = _output_phase(xcur_s[grp_rows, :], gate, o_s, ga_s, yb_s, ra_s, rb_s, gn_ref.at[0],
                          wa_ref.at[0], wb_ref.at[0], wo_ref.at[0], lng_ref.at[0], lnb_ref.at[0])
        y_ref[0] = y
        xcur_s[grp_rows, :] = y
        for j in range(N_TILES):
            for i in range(CONV_W - 1):
                cout_ref[0, i, :, j * LANE_TILE:(j + 1) * LANE_TILE] = u_s[j, tok(SAMPLE_T - (CONV_W - 1) + i), :]


def _sample_trunk(x, ada, conv_in, params, weights, state):
    n_rows_all = x.shape[0]
    n_seq = state.shape[1]
    n_grp = n_seq // SAMPLE_GROUP_SEQ
    grp_rows = SAMPLE_GROUP_SEQ * SAMPLE_T
    steps = SAMPLE_GROUP_SEQ // SAMPLE_SEQ_PER_STEP
    grid = (DEPTH, n_grp, steps)
    state_blk = (1, SAMPLE_SEQ_PER_STEP, N_HEADS, HEAD_DIM, HEAD_DIM)
    state_idx = lambda l, g, s: (l, g * steps + s, 0, 0, 0)
    conv_blk = (1, CONV_W - 1, SAMPLE_GROUP_SEQ, D_MODEL)
    conv_idx = lambda l, g, s: (l, 0, g, 0)
    at_layer = lambda l, g, s: l
    out_shape = (
        jax.ShapeDtypeStruct((DEPTH, n_rows_all, D_MODEL), F32),
        jax.ShapeDtypeStruct(state.shape, F32),
        jax.ShapeDtypeStruct(conv_in.shape, F32),
    )
    in_specs = ([pl.BlockSpec((grp_rows, D_MODEL), lambda l, g, s: (g, 0)),
                 pl.BlockSpec((1, SAMPLE_GROUP_SEQ, 3 * D_MODEL), lambda l, g, s: (l, g, 0)),
                 pl.BlockSpec(conv_blk, conv_idx)]
                + [_layer_spec(p, at_layer) for p in params]
                + [_layer_spec(w, at_layer) for w in weights]
                + [pl.BlockSpec(state_blk, state_idx)])
    out_specs = (
        pl.BlockSpec((1, grp_rows, D_MODEL), lambda l, g, s: (l, g, 0)),
        pl.BlockSpec(state_blk, state_idx),
        pl.BlockSpec(conv_blk, conv_idx),
    )
    scratch = ([pltpu.VMEM((n_rows_all, D_MODEL), F32), pltpu.VMEM((grp_rows, D_MODEL), BF16)]
               + _tile_scratch(grp_rows)
               + [pltpu.VMEM((N_HEADS, HEAD_DIM, grp_rows), BF16),
                  pltpu.VMEM((N_HEADS, HEAD_DIM, grp_rows), F32)])
    return pl.pallas_call(
        _sample_kernel,
        grid=grid,
        in_specs=in_specs,
        out_specs=out_specs,
        out_shape=out_shape,
        scratch_shapes=scratch,
        compiler_params=pltpu.CompilerParams(
            dimension_semantics=("arbitrary", "arbitrary", "arbitrary"),
            vmem_limit_bytes=VMEM_LIMIT_BYTES),
        name="sample_trunk",
    )(x, ada, conv_in, *params, *weights, state)


def _prep_kernel(c_ref, wada_ref, bada_ref, lbp_ref, ada_ref, lbs_ref):
    c = c_ref[...]
    sc = (c * _sigmoid(c)).astype(BF16)
    ada_ref[0] = jnp.dot(sc, wada_ref[0], preferred_element_type=F32) + bada_ref[0]
    p = [lbp_ref[i:i + 1, :] for i in range(DEPTH)]
    mx = functools.reduce(jnp.maximum, p)
    e = [jnp.exp(pi - mx) for pi in p]
    tot = functools.reduce(lambda a, b: a + b, e)
    acc = jnp.zeros_like(mx)
    lbs_ref[0:1, :] = acc
    for i in range(1, DEPTH):
        acc = acc + e[i] / tot
        lbs_ref[i:i + 1, :] = acc


def _prep(c_all, w_ada, b_ada, lb_param):
    n = c_all.shape[0]
    return pl.pallas_call(
        _prep_kernel,
        grid=(DEPTH,),
        in_specs=[
            pl.BlockSpec((n, D_MODEL), lambda l: (0, 0)),
            pl.BlockSpec((1, D_MODEL, 3 * D_MODEL), lambda l: (l, 0, 0)),
            pl.BlockSpec((1, 1, 3 * D_MODEL), lambda l: (l, 0, 0)),
            pl.BlockSpec((DEPTH, D_MODEL), lambda l: (0, 0)),
        ],
        out_specs=(
            pl.BlockSpec((1, n, 3 * D_MODEL), lambda l: (l, 0, 0)),
            pl.BlockSpec((DEPTH, D_MODEL), lambda l: (0, 0)),
        ),
        out_shape=(
            jax.ShapeDtypeStruct((DEPTH, n, 3 * D_MODEL), F32),
            jax.ShapeDtypeStruct((DEPTH, D_MODEL), F32),
        ),
        compiler_params=pltpu.CompilerParams(dimension_semantics=("arbitrary",)),
        name="conditioning",
    )(c_all, w_ada, b_ada.reshape(DEPTH, 1, 3 * D_MODEL), lb_param)


def _tiled(p):
    return p.reshape(DEPTH, p.shape[1], N_TILES, LANE_TILE).transpose(0, 2, 1, 3)


def kernel(x_prompt, x_sample, state_hgrn, state_conv, c_prompt, c_sample, w_ada, b_ada, w_in, hgrn_lb,
           hgrn_gnorm, conv_w, w_a_out, w_b_out, w_o, ln_g, ln_b):
    n_prompt = x_prompt.shape[0]
    n_seq, t_dec, _ = x_sample.shape
    assert t_dec == SAMPLE_T and n_seq % SAMPLE_GROUP_SEQ == 0 and x_prompt.shape[1] % CHUNK == 0
    n_grp = n_seq // SAMPLE_GROUP_SEQ

    c_all = jnp.concatenate([c_prompt, c_sample], axis=0)
    ada_all, lbs = _prep(c_all, w_ada.astype(BF16), b_ada, hgrn_lb)

    params = (_tiled(lbs[:, None, :]), _tiled(hgrn_gnorm[:, None, :]), _tiled(conv_w),
              ln_g[:, None, :], ln_b[:, None, :])
    weights = (w_in.astype(BF16),
               w_a_out.astype(BF16).reshape(DEPTH, N_TILES, LANE_TILE, D_MODEL),
               w_b_out.astype(BF16).reshape(DEPTH, N_TILES, LANE_TILE, D_MODEL),
               w_o.astype(BF16).reshape(DEPTH, N_TILES, LANE_TILE, D_MODEL))

    xp = x_prompt
    hgrn_p, conv_p = [], []
    for l in range(DEPTH):
        ada_p = ada_all[l, :n_prompt].reshape(n_prompt, 1, 3 * D_MODEL)
        xp, s_new, c_new = _prompt_layer(l, xp, ada_p, params, weights)
        hgrn_p.append(s_new)
        conv_p.append(c_new)

    xs = (x_sample.reshape(n_grp, SAMPLE_GROUP_SEQ, t_dec, D_MODEL).transpose(0, 2, 1, 3)
          .reshape(n_seq * t_dec, D_MODEL))
    ys, hgrn_s, conv_s = _sample_trunk(xs, ada_all[:, n_prompt:], state_conv.transpose(0, 2, 1, 3),
                                       params, weights, state_hgrn)
    ys = (ys[DEPTH - 1].reshape(n_grp, t_dec, SAMPLE_GROUP_SEQ, D_MODEL).transpose(0, 2, 1, 3)
          .reshape(n_seq, t_dec, D_MODEL))

    return (xp, ys, jnp.stack(hgrn_p), jnp.stack(conv_p), hgrn_s, conv_s.transpose(0, 2, 1, 3))
```

```python
import functools
import itertools

import jax
import jax.numpy as jnp
from jax import lax
from jax.experimental import pallas as pl
from jax.experimental.pallas import tpu as pltpu

F32 = jnp.float32
BF16 = jnp.bfloat16

D_MODEL = 1024
N_HEADS = 8
HEAD_DIM = 128
N_GROUPS = 10
LANE_TILE = 256
HEADS_PER_TILE = LANE_TILE // HEAD_DIM
N_TILES = D_MODEL // LANE_TILE
CONV_W = 3
CHUNK = 64
DEPTH = 4
DEEPNORM_ALPHA = (2 * DEPTH) ** 0.25
LN_EPS = 1e-5
RMS_EPS = 1e-6
LB_FLOOR = 1e-30
Q_SCALE = HEAD_DIM ** -0.5
LOG2_E = 1.4426950408889634
TREE_LEVELS = (1, 2, 4, 8, 16, 32)
PROMPT_BLOCK_T = 512
SAMPLE_T = 4
SAMPLE_GROUP_SEQ = 32
SAMPLE_SEQ_PER_STEP = 8
VMEM_LIMIT_BYTES = 64 * 1024 * 1024 - 256 * 1024

_NT = (((1,), (1,)), ((), ()))
_TN = (((0,), (0,)), ((), ()))


def _sigmoid(x):
    return 1.0 / (1.0 + jnp.exp2(x * (-LOG2_E)))


def _roll_rows(x, shift):
    return pltpu.roll(x, shift % x.shape[0], axis=0)


def _roll_in_groups(x, shift):
    n, w = x.shape
    return pltpu.roll(x.reshape(n // 8, 8, w), shift % 8, axis=1).reshape(n, w)


def _halves(x, h):
    n = x.shape[0]
    first = [x[b:b + h] for b in range(0, n, 2 * h)]
    second = [x[b + h:b + 2 * h] for b in range(0, n, 2 * h)]
    return first, second


def _interleave(first, second):
    out = []
    for a, b in zip(first, second):
        out += [a, b]
    return jnp.concatenate(out, axis=0)


def _tree_factors(q, k, lf, levels, row):
    p = lf
    tot = lf
    xs = []
    for h in levels:
        if h < 8:
            second = (row & h) != 0
            other = _roll_in_groups(tot, h)
            if h != 4:
                other = jnp.where(second, other, _roll_in_groups(tot, -h))
            x = jnp.where(second, q, k) * jnp.exp2(jnp.where(second, p, tot - p))
            p = p + jnp.where(second, other, 0.0)
            tot = tot + other
        else:
            p1, p2 = _halves(p, h)
            t1, t2 = _halves(tot, h)
            k1, _ = _halves(k, h)
            _, q2 = _halves(q, h)
            x = _interleave([kk * jnp.exp2(tt - pp) for kk, tt, pp in zip(k1, t1, p1)],
                            [qq * jnp.exp2(pp) for qq, pp in zip(q2, p2)])
            p = _interleave(p1, [pp + tt for pp, tt in zip(p2, t1)])
            both = [a + b for a, b in zip(t1, t2)]
            tot = _interleave(both, both)
        xs.append(x.astype(BF16))
    return xs, p, tot


def _pair_level_code(n, levels):
    t = lax.broadcasted_iota(jnp.int32, (n, 2 * n), 0)
    s = lax.broadcasted_iota(jnp.int32, (n, 2 * n), 1) & (n - 1)
    x = t ^ s
    code = jnp.where(s == t, 0, -1)
    for i, h in enumerate(levels):
        code = jnp.where((s < t) & (x >= h) & (x < 2 * h), i + 1, code)
    return code


def _block_diag(x):
    w = x.shape[1] // 2
    z = jnp.zeros((x.shape[0], w), x.dtype)
    return jnp.concatenate([jnp.concatenate([x[:, :w], z], axis=1),
                            jnp.concatenate([z, x[:, w:]], axis=1)], axis=0)


def _block_diag_t(x):
    n = x.shape[0]
    w = x.shape[1] // 2
    z = jnp.zeros((w - n, w), x.dtype)
    t0 = jnp.concatenate([x[:, :w], z], axis=0).T
    t1 = jnp.concatenate([z, x[:, w:]], axis=0).T
    return jnp.concatenate([t0, t1], axis=0)


def _pair_grams(xs):
    return [jnp.dot(x, _block_diag_t(x), preferred_element_type=F32) for x in xs]


def _pair_scores(grams, levels, qk, code):
    n = qk.shape[0]
    w = qk.shape[1] // 2
    lane = lax.broadcasted_iota(jnp.int32, (1, 2 * n), 1)
    diag = jnp.where(lane < n, jnp.sum(qk[:, :w], axis=1, keepdims=True),
                     jnp.sum(qk[:, w:], axis=1, keepdims=True))
    pieces = []
    for r0 in range(0, n, 8):
        rows = slice(r0, r0 + 8)
        c = code[rows]
        a = jnp.where(c == 0, diag[rows], 0.0)
        for i, h in enumerate(levels):
            if h < 8 or (r0 & h):
                a = jnp.where(c == i + 1, grams[i][rows], a)
        pieces.append(a)
    return jnp.concatenate(pieces, axis=0)


def _proj(hb, win_ref, g, j):
    col = g * D_MODEL + j * LANE_TILE
    return jnp.dot(hb, win_ref[:, col:col + LANE_TILE], preferred_element_type=F32)


def _project_recurrent(j, hb, win_ref, lb_ref, q_s, k_s, lf_s, v_s):
    qv = _proj(hb, win_ref, 0, j)
    q_s[j] = qv * _sigmoid(qv) * Q_SCALE
    lb = lb_ref[j]
    one_m = 1.0 - lb
    sig = _sigmoid(_proj(hb, win_ref, 1, j))
    lf_s[j] = jnp.log(jnp.maximum(lb, LB_FLOOR) + one_m * sig) * LOG2_E
    k_s[j] = one_m * (1.0 - sig)
    v_s[j] = _proj(hb, win_ref, 2, j).astype(BF16)


def _project_gates(j, hb, win_ref, cw_ref, delayed, ga_s, yb_s, ra_s, rb_s, u_s):
    proj = functools.partial(_proj, hb, win_ref, j=j)
    ga = proj(3)
    ga_s[j] = ga * _sigmoid(ga)
    yield ga
    cg = proj(5)
    yield cg
    hv = proj(6)
    u = cg * hv
    cw = cw_ref[j]
    u1, u2 = delayed(u)
    conv = u2 * cw[0:1, :] + u1 * cw[1:2, :] + u * cw[2:3, :]
    u_s[j] = u
    yield hv
    bg = proj(4)
    conv = conv * bg
    yield bg
    gb = proj(7)
    yb_s[j] = (conv * (gb * _sigmoid(gb))).astype(BF16)
    yield gb
    ra = proj(8)
    ra_s[j] = _sigmoid(ra)
    yield ra
    rb = proj(9)
    rb_s[j] = _sigmoid(rb)
    yield rb


def _output_phase(x, gate, o_s, ga_s, yb_s, ra_s, rb_s, gn_ref, wa_ref, wb_ref, wo_ref, lng_ref, lnb_ref):
    ya = None
    yb = None
    for j in range(N_TILES):
        o = o_s[j]
        parts = []
        for hh in range(HEADS_PER_TILE):
            oh = o[:, hh * HEAD_DIM:(hh + 1) * HEAD_DIM]
            ms = jnp.mean(oh * oh, axis=1, keepdims=True)
            parts.append(oh * lax.rsqrt(ms + RMS_EPS))
        on = jnp.concatenate(parts, axis=1) * gn_ref[j]
        da = jnp.dot((on * ga_s[j]).astype(BF16), wa_ref[j], preferred_element_type=F32)
        db = jnp.dot(yb_s[j], wb_ref[j], preferred_element_type=F32)
        ya = da if ya is None else ya + da
        yb = db if yb is None else yb + db
    out = None
    for j in range(N_TILES):
        lanes = slice(j * LANE_TILE, (j + 1) * LANE_TILE)
        m = ra_s[j] * ya[:, lanes] + rb_s[j] * yb[:, lanes]
        dm = jnp.dot(m.astype(BF16), wo_ref[j], preferred_element_type=F32)
        out = dm if out is None else out + dm
    res = DEEPNORM_ALPHA * x + gate * out
    mu = jnp.mean(res, axis=1, keepdims=True)
    cen = res - mu
    var = jnp.mean(cen * cen, axis=1, keepdims=True)
    return cen * lax.rsqrt(var + LN_EPS) * lng_ref[...] + lnb_ref[...]


def _modulated_input(x, ada):
    shift = ada[:, 0:D_MODEL]
    scale = ada[:, D_MODEL:2 * D_MODEL]
    gate = ada[:, 2 * D_MODEL:3 * D_MODEL]
    return x * (1.0 + scale) + shift, gate


def _prompt_layer_kernel(x_ref, ada_ref, lb_ref, gn_ref, cw_ref, lng_ref, lnb_ref,
                         win_ref, wa_ref, wb_ref, wo_ref,
                         y_ref, sfin_ref, cfin_ref,
                         hb_s, q_s, k_s, lf_s, v_s, ga_s, yb_s, ra_s, rb_s, u_s, o_s, st_s, cprev_s):
    t_blk = pl.program_id(1)
    tb = x_ref.shape[1]

    @pl.when(t_blk == 0)
    def _():
        st_s[...] = jnp.zeros_like(st_s)
        cprev_s[...] = jnp.zeros_like(cprev_s)

    x = x_ref[0]
    h, gate = _modulated_input(x, ada_ref[0])
    hb_s[...] = h.astype(BF16)

    rows_tb = lax.broadcasted_iota(jnp.int32, (tb, 1), 0)

    for j in range(N_TILES):
        _project_recurrent(j, hb_s[...], win_ref.at[0], lb_ref.at[0], q_s, k_s, lf_s, v_s)

    row = lax.broadcasted_iota(jnp.int32, (CHUNK, 1), 0)
    code = _pair_level_code(CHUNK, TREE_LEVELS)

    def gates(j):
        prev = cprev_s[j]

        def delayed(u):
            u1 = jnp.where(rows_tb == 0, prev[7:8, :], _roll_rows(u, 1))
            u2 = jnp.where(rows_tb == 0, prev[6:7, :],
                           jnp.where(rows_tb == 1, prev[7:8, :], _roll_rows(u, 2)))
            return u1, u2

        yield from _project_gates(j, hb_s[...], win_ref.at[0], cw_ref.at[0], delayed,
                                  ga_s, yb_s, ra_s, rb_s, u_s)
        cprev_s[j] = u_s[j, tb - 8:tb, :]

    filler = itertools.chain.from_iterable(gates(j) for j in range(N_TILES))

    def stage_front(c, j):
        rows = pl.ds(c * CHUNK, CHUNK)
        q = q_s[j, rows, :]
        k = k_s[j, rows, :]
        v = v_s[j, rows, :]
        xs, gc, tot = _tree_factors(q, k, lf_s[j, rows, :], TREE_LEVELS, row)
        grams = _pair_grams(xs)
        next(filler, None)
        return rows, j, q, k, v, gc, tot, grams

    def stage_back(rows, j, q, k, v, gc, tot, grams):
        dec = jnp.exp2(tot[0:1, :])
        qg = (q * jnp.exp2(gc)).astype(BF16)
        kd = (k * jnp.exp2(tot - gc)).astype(BF16)
        a = _pair_scores(grams, TREE_LEVELS, q * k, code)
        o = jnp.dot(a.astype(BF16), _block_diag(v), preferred_element_type=F32)
        inter = []
        for hh in range(HEADS_PER_TILE):
            lanes = slice(hh * HEAD_DIM, (hh + 1) * HEAD_DIM)
            hd = j * HEADS_PER_TILE + hh
            st = st_s[hd]
            inter.append(lax.dot_general(qg[:, lanes], st.astype(BF16), _NT,
                                         preferred_element_type=F32))
            st_s[hd] = st * dec[:, lanes] + lax.dot_general(v[:, lanes], kd[:, lanes], _TN,
                                                             preferred_element_type=F32)
        next(filler, None)
        o_s[j, rows, :] = o + jnp.concatenate(inter, axis=1)

    pending = None
    for c in range(tb // CHUNK):
        for j in range(N_TILES):
            front = stage_front(c, j)
            if pending is not None:
                stage_back(*pending)
            pending = front
    stage_back(*pending)
    for _ in filler:
        pass

    y_ref[0] = _output_phase(x, gate, o_s, ga_s, yb_s, ra_s, rb_s, gn_ref.at[0], wa_ref.at[0],
                             wb_ref.at[0], wo_ref.at[0], lng_ref.at[0], lnb_ref.at[0])

    @pl.when(t_blk == pl.num_programs(1) - 1)
    def _():
        for hd in range(N_HEADS):
            sfin_ref[0, hd] = st_s[hd].T
        for j in range(N_TILES):
            cfin_ref[0, :, j * LANE_TILE:(j + 1) * LANE_TILE] = cprev_s[j, 8 - (CONV_W - 1):8, :]


def _layer_spec(arr, index_of_layer):
    nd = arr.ndim
    return pl.BlockSpec((1,) + arr.shape[1:], lambda *ids: (index_of_layer(*ids),) + (0,) * (nd - 1),
                        pipeline_mode=pl.Buffered(1))


def _tile_scratch(rows):
    f = lambda dt: pltpu.VMEM((N_TILES, rows, LANE_TILE), dt)
    return [f(F32), f(F32), f(F32), f(BF16), f(F32), f(BF16), f(F32), f(F32), f(F32), f(F32)]


def _prompt_layer(layer, x, ada, params, weights):
    bsz, seq, _ = x.shape
    block_t = min(PROMPT_BLOCK_T, seq)
    grid = (bsz, seq // block_t)
    out_shape = (
        jax.ShapeDtypeStruct((bsz, seq, D_MODEL), F32),
        jax.ShapeDtypeStruct((bsz, N_HEADS, HEAD_DIM, HEAD_DIM), F32),
        jax.ShapeDtypeStruct((bsz, CONV_W - 1, D_MODEL), F32),
    )
    at_layer = lambda b, t: layer
    in_specs = ([pl.BlockSpec((1, block_t, D_MODEL), lambda b, t: (b, t, 0)),
                 pl.BlockSpec((1, 1, 3 * D_MODEL), lambda b, t: (b, 0, 0))]
                + [_layer_spec(p, at_layer) for p in params]
                + [_layer_spec(w, at_layer) for w in weights])
    out_specs = (
        pl.BlockSpec((1, block_t, D_MODEL), lambda b, t: (b, t, 0)),
        pl.BlockSpec((1, N_HEADS, HEAD_DIM, HEAD_DIM), lambda b, t: (b, 0, 0, 0)),
        pl.BlockSpec((1, CONV_W - 1, D_MODEL), lambda b, t: (b, 0, 0)),
    )
    scratch = ([pltpu.VMEM((block_t, D_MODEL), BF16)] + _tile_scratch(block_t)
               + [pltpu.VMEM((N_HEADS, HEAD_DIM, HEAD_DIM), F32),
                  pltpu.VMEM((N_TILES, 8, LANE_TILE), F32)])
    return pl.pallas_call(
        _prompt_layer_kernel,
        grid=grid,
        in_specs=in_specs,
        out_specs=out_specs,
        out_shape=out_shape,
        scratch_shapes=scratch,
        compiler_params=pltpu.CompilerParams(
            dimension_semantics=("arbitrary", "arbitrary"),
            vmem_limit_bytes=VMEM_LIMIT_BYTES,
        ),
        name="prompt_layer",
    )(x, ada, *params, *weights)


def _sample_kernel(x_ref, ada_ref, cin_ref, lb_ref, gn_ref, cw_ref, lng_ref, lnb_ref,
                   win_ref, wa_ref, wb_ref, wo_ref, sin_ref,
                   y_ref, sout_ref, cout_ref,
                   xcur_s, hb_s, q_s, k_s, lf_s, v_s, ga_s, yb_s, ra_s, rb_s, u_s, o_s, kdt_s, dct_s):
    layer = pl.program_id(0)
    grp = pl.program_id(1)
    step = pl.program_id(2)
    nseq = SAMPLE_GROUP_SEQ
    n_rows = nseq * SAMPLE_T
    grp_rows = pl.ds(pl.multiple_of(grp * n_rows, n_rows), n_rows)
    tok = lambda t: slice(t * nseq, (t + 1) * nseq)

    @pl.when(step == 0)
    def _():
        @pl.when(layer == 0)
        def _():
            xcur_s[grp_rows, :] = x_ref[...]

        ada = ada_ref[0]
        x = xcur_s[grp_rows, :]
        for t in range(SAMPLE_T):
            h, _ = _modulated_input(x[tok(t)], ada)
            hb_s[tok(t), :] = h.astype(BF16)

        for j in range(N_TILES):
            lanes_j = slice(j * LANE_TILE, (j + 1) * LANE_TILE)
            hist0 = cin_ref[0, 0, :, lanes_j]
            hist1 = cin_ref[0, 1, :, lanes_j]

            def delayed(u):
                return (jnp.concatenate([hist1, u[:(SAMPLE_T - 1) * nseq]], axis=0),
                        jnp.concatenate([hist0, hist1, u[:(SAMPLE_T - 2) * nseq]], axis=0))

            _project_recurrent(j, hb_s[...], win_ref.at[0], lb_ref.at[0], q_s, k_s, lf_s, v_s)
            for _ in _project_gates(j, hb_s[...], win_ref.at[0], cw_ref.at[0], delayed,
                                    ga_s, yb_s, ra_s, rb_s, u_s):
                pass

            q = [q_s[j, tok(t), :] for t in range(SAMPLE_T)]
            k = [k_s[j, tok(t), :] for t in range(SAMPLE_T)]
            v = [v_s[j, tok(t), :].astype(F32) for t in range(SAMPLE_T)]
            gc = [lf_s[j, tok(0), :]]
            for t in range(1, SAMPLE_T):
                gc.append(gc[-1] + lf_s[j, tok(t), :])
            tot = gc[-1]
            for t in range(SAMPLE_T):
                o_heads = [jnp.zeros((nseq, HEAD_DIM), F32) for _ in range(HEADS_PER_TILE)]
                for s in range(t + 1):
                    w = q[t] * k[s]
                    if s < t:
                        w = w * jnp.exp2(gc[t] - gc[s])
                    for hh in range(HEADS_PER_TILE):
                        lanes = slice(hh * HEAD_DIM, (hh + 1) * HEAD_DIM)
                        a = jnp.sum(w[:, lanes], axis=1, keepdims=True)
                        o_heads[hh] = o_heads[hh] + a * v[s][:, lanes]
                o_s[j, tok(t), :] = jnp.concatenate(o_heads, axis=1)
            for t in range(SAMPLE_T):
                q_s[j, tok(t), :] = q[t] * jnp.exp2(gc[t])
                k_s[j, tok(t), :] = k[t] * jnp.exp2(tot - gc[t])
                lf_s[j, tok(t), :] = jnp.exp2(tot)
            for hh in range(HEADS_PER_TILE):
                lanes = slice(hh * HEAD_DIM, (hh + 1) * HEAD_DIM)
                hd = j * HEADS_PER_TILE + hh
                kdt_s[hd] = k_s[j, :, lanes].T.astype(BF16)
                dct_s[hd] = lf_s[j, :, lanes].T

    seq0 = step * SAMPLE_SEQ_PER_STEP
    piece0 = pl.multiple_of((seq0 // 8) * 8, 8)
    lane_id = lax.broadcasted_iota(jnp.int32, (1, n_rows), 1)
    sub_id = lax.broadcasted_iota(jnp.int32, (8 * SAMPLE_T, 1), 0) & 7

    def seq_state(j, carry):
        for hh in range(HEADS_PER_TILE):
            lanes = slice(hh * HEAD_DIM, (hh + 1) * HEAD_DIM)
            hd = j * HEADS_PER_TILE + hh
            qg = jnp.concatenate([q_s[j, pl.ds(t * nseq + piece0, 8), lanes] for t in range(SAMPLE_T)],
                                 axis=0).astype(BF16)
            vblk = v_s[j, :, lanes]
            kdt = kdt_s[hd]
            dct = dct_s[hd]
            o_acc = jnp.zeros((8 * SAMPLE_T, HEAD_DIM), F32)
            for s in range(SAMPLE_SEQ_PER_STEP):
                seq = seq0 + s
                st = sin_ref[0, s, hd]
                res = jnp.dot(qg, st.astype(BF16), preferred_element_type=F32)
                o_acc = o_acc + jnp.where(sub_id == (seq & 7), res, 0.0)
                mine = (lane_id & (nseq - 1)) == seq
                upd = jnp.dot(jnp.where(mine, kdt, jnp.zeros_like(kdt)), vblk, preferred_element_type=F32)
                dcol = jnp.sum(jnp.where(lane_id == seq, dct, 0.0), axis=1, keepdims=True)
                sout_ref[0, s, hd] = st * dcol + upd
            for t in range(SAMPLE_T):
                rows = pl.ds(t * nseq + piece0, 8)
                o_s[j, rows, lanes] = o_s[j, rows, lanes] + o_acc[8 * t:8 * (t + 1)]
        return carry

    lax.fori_loop(0, N_TILES, seq_state, 0)

    @pl.when(step == pl.num_programs(2) - 1)
    def _():
        ada = ada_ref[0]
        gate = jnp.concatenate([ada[:, 2 * D_MODEL:3 * D_MODEL]] * SAMPLE_T, axis=0)
        y = _output_phase(xcur_s[grp_rows, :], gate, o_s, ga_s, yb_s, ra_s, rb_s, gn_ref.at[0],
                          wa_ref.at[0], wb_ref.at[0], wo_ref.at[0], lng_ref.at[0], lnb_ref.at[0])
        y_ref[0] = y
        xcur_s[grp_rows, :] = y
        for j in range(N_TILES):
            for i in range(CONV_W - 1):
                cout_ref[0, i, :, j * LANE_TILE:(j + 1) * LANE_TILE] = u_s[j, tok(SAMPLE_T - (CONV_W - 1) + i), :]


def _sample_trunk(x, ada, conv_in, params, weights, state):
    n_rows_all = x.shape[0]
    n_seq = state.shape[1]
    n_grp = n_seq // SAMPLE_GROUP_SEQ
    grp_rows = SAMPLE_GROUP_SEQ * SAMPLE_T
    steps = SAMPLE_GROUP_SEQ // SAMPLE_SEQ_PER_STEP
    grid = (DEPTH, n_grp, steps)
    state_blk = (1, SAMPLE_SEQ_PER_STEP, N_HEADS, HEAD_DIM, HEAD_DIM)
    state_idx = lambda l, g, s: (l, g * steps + s, 0, 0, 0)
    conv_blk = (1, CONV_W - 1, SAMPLE_GROUP_SEQ, D_MODEL)
    conv_idx = lambda l, g, s: (l, 0, g, 0)
    at_layer = lambda l, g, s: l
    out_shape = (
        jax.ShapeDtypeStruct((DEPTH, n_rows_all, D_MODEL), F32),
        jax.ShapeDtypeStruct(state.shape, F32),
        jax.ShapeDtypeStruct(conv_in.shape, F32),
    )
    in_specs = ([pl.BlockSpec((grp_rows, D_MODEL), lambda l, g, s: (g, 0)),
                 pl.BlockSpec((1, SAMPLE_GROUP_SEQ, 3 * D_MODEL), lambda l, g, s: (l, g, 0)),
                 pl.BlockSpec(conv_blk, conv_idx)]
                + [_layer_spec(p, at_layer) for p in params]
                + [_layer_spec(w, at_layer) for w in weights]
                + [pl.BlockSpec(state_blk, state_idx)])
    out_specs = (
        pl.BlockSpec((1, grp_rows, D_MODEL), lambda l, g, s: (l, g, 0)),
        pl.BlockSpec(state_blk, state_idx),
        pl.BlockSpec(conv_blk, conv_idx),
    )
    scratch = ([pltpu.VMEM((n_rows_all, D_MODEL), F32), pltpu.VMEM((grp_rows, D_MODEL), BF16)]
               + _tile_scratch(grp_rows)
               + [pltpu.VMEM((N_HEADS, HEAD_DIM, grp_rows), BF16),
                  pltpu.VMEM((N_HEADS, HEAD_DIM, grp_rows), F32)])
    return pl.pallas_call(
        _sample_kernel,
        grid=grid,
        in_specs=in_specs,
        out_specs=out_specs,
        out_shape=out_shape,
        scratch_shapes=scratch,
        compiler_params=pltpu.CompilerParams(
            dimension_semantics=("arbitrary", "arbitrary", "arbitrary"),
            vmem_limit_bytes=VMEM_LIMIT_BYTES),
        name="sample_trunk",
    )(x, ada, conv_in, *params, *weights, state)


def _prep_kernel(c_ref, wada_ref, bada_ref, lbp_ref, ada_ref, lbs_ref):
    c = c_ref[...]
    sc = (c * _sigmoid(c)).astype(BF16)
    ada_ref[0] = jnp.dot(sc, wada_ref[0].astype(BF16), preferred_element_type=F32) + bada_ref[0]
    p = [lbp_ref[i:i + 1, :] for i in range(DEPTH)]
    mx = functools.reduce(jnp.maximum, p)
    e = [jnp.exp(pi - mx) for pi in p]
    tot = functools.reduce(lambda a, b: a + b, e)
    acc = jnp.zeros_like(mx)
    lbs_ref[0:1, :] = acc
    for i in range(1, DEPTH):
        acc = acc + e[i] / tot
        lbs_ref[i:i + 1, :] = acc


def _prep(c_all, w_ada, b_ada, lb_param):
    n = c_all.shape[0]
    return pl.pallas_call(
        _prep_kernel,
        grid=(DEPTH,),
        in_specs=[
            pl.BlockSpec((n, D_MODEL), lambda l: (0, 0)),
            pl.BlockSpec((1, D_MODEL, 3 * D_MODEL), lambda l: (l, 0, 0)),
            pl.BlockSpec((1, 1, 3 * D_MODEL), lambda l: (l, 0, 0)),
            pl.BlockSpec((DEPTH, D_MODEL), lambda l: (0, 0)),
        ],
        out_specs=(
            pl.BlockSpec((1, n, 3 * D_MODEL), lambda l: (l, 0, 0)),
            pl.BlockSpec((DEPTH, D_MODEL), lambda l: (0, 0)),
        ),
        out_shape=(
            jax.ShapeDtypeStruct((DEPTH, n, 3 * D_MODEL), F32),
            jax.ShapeDtypeStruct((DEPTH, D_MODEL), F32),
        ),
        compiler_params=pltpu.CompilerParams(dimension_semantics=("arbitrary",)),
        name="conditioning",
    )(c_all, w_ada, b_ada.reshape(DEPTH, 1, 3 * D_MODEL), lb_param)


def _tiled(p):
    return p.reshape(DEPTH, p.shape[1], N_TILES, LANE_TILE).transpose(0, 2, 1, 3)


def kernel(x_prompt, x_sample, state_hgrn, state_conv, c_prompt, c_sample, w_ada, b_ada, w_in, hgrn_lb,
           hgrn_gnorm, conv_w, w_a_out, w_b_out, w_o, ln_g, ln_b):
    n_prompt = x_prompt.shape[0]
    n_seq, t_dec, _ = x_sample.shape
    assert t_dec == SAMPLE_T and n_seq % SAMPLE_GROUP_SEQ == 0 and x_prompt.shape[1] % CHUNK == 0
    n_grp = n_seq // SAMPLE_GROUP_SEQ

    c_all = jnp.concatenate([c_prompt, c_sample], axis=0)
    ada_all, lbs = _prep(c_all, w_ada, b_ada, hgrn_lb)

    params = (_tiled(lbs[:, None, :]), _tiled(hgrn_gnorm[:, None, :]), _tiled(conv_w),
              ln_g[:, None, :], ln_b[:, None, :])
    weights = (w_in.astype(BF16),
               w_a_out.astype(BF16).reshape(DEPTH, N_TILES, LANE_TILE, D_MODEL),
               w_b_out.astype(BF16).reshape(DEPTH, N_TILES, LANE_TILE, D_MODEL),
               w_o.astype(BF16).reshape(DEPTH, N_TILES, LANE_TILE, D_MODEL))

    xp = x_prompt
    hgrn_p, conv_p = [], []
    for l in range(DEPTH):
        ada_p = ada_all[l, :n_prompt].reshape(n_prompt, 1, 3 * D_MODEL)
        xp, s_new, c_new = _prompt_layer(l, xp, ada_p, params, weights)
        hgrn_p.append(s_new)
        conv_p.append(c_new)

    xs = (x_sample.reshape(n_grp, SAMPLE_GROUP_SEQ, t_dec, D_MODEL).transpose(0, 2, 1, 3)
          .reshape(n_seq * t_dec, D_MODEL))
    ys, hgrn_s, conv_s = _sample_trunk(xs, ada_all[:, n_prompt:], state_conv.transpose(0, 2, 1, 3),
                                       params, weights, state_hgrn)
    ys = (ys[DEPTH - 1].reshape(n_grp, t_dec, SAMPLE_GROUP_SEQ, D_MODEL).transpose(0, 2, 1, 3)
          .reshape(n_seq, t_dec, D_MODEL))

    return (xp, ys, jnp.stack(hgrn_p), jnp.stack(conv_p), hgrn_s, conv_s.transpose(0, 2, 1, 3))
```

```python
import functools
import itertools

import jax
import jax.numpy as jnp
from jax import lax
from jax.experimental import pallas as pl
from jax.experimental.pallas import tpu as pltpu

F32 = jnp.float32
BF16 = jnp.bfloat16

D_MODEL = 1024
N_HEADS = 8
HEAD_DIM = 128
N_GROUPS = 10
LANE_TILE = 256
HEADS_PER_TILE = LANE_TILE // HEAD_DIM
N_TILES = D_MODEL // LANE_TILE
CONV_W = 3
CHUNK = 64
DEPTH = 4
DEEPNORM_ALPHA = (2 * DEPTH) ** 0.25
LN_EPS = 1e-5
RMS_EPS = 1e-6
LB_FLOOR = 1e-30
Q_SCALE = HEAD_DIM ** -0.5
LOG2_E = 1.4426950408889634
TREE_LEVELS = (1, 2, 4, 8, 16, 32)
PROMPT_BLOCK_T = 512
SAMPLE_T = 4
SAMPLE_GROUP_SEQ = 32
SAMPLE_SEQ_PER_STEP = 8
VMEM_LIMIT_BYTES = 64 * 1024 * 1024 - 256 * 1024

_TN = (((0,), (0,)), ((), ()))


def _sigmoid(x):
    return 1.0 / (1.0 + jnp.exp2(x * (-LOG2_E)))


def _roll_rows(x, shift):
    return pltpu.roll(x, shift % x.shape[0], axis=0)


def _roll_in_groups(x, shift):
    n, w = x.shape
    return pltpu.roll(x.reshape(n // 8, 8, w), shift % 8, axis=1).reshape(n, w)


def _halves(x, h):
    n = x.shape[0]
    first = [x[b:b + h] for b in range(0, n, 2 * h)]
    second = [x[b + h:b + 2 * h] for b in range(0, n, 2 * h)]
    return first, second


def _interleave(first, second):
    out = []
    for a, b in zip(first, second):
        out += [a, b]
    return jnp.concatenate(out, axis=0)


def _tree_factors(q, k, lf, levels, row):
    p = lf
    tot = lf
    xs = []
    for h in levels:
        if h < 8:
            second = (row & h) != 0
            other = _roll_in_groups(tot, h)
            if h != 4:
                other = jnp.where(second, other, _roll_in_groups(tot, -h))
            x = jnp.where(second, q, k) * jnp.exp2(jnp.where(second, p, tot - p))
            p = p + jnp.where(second, other, 0.0)
            tot = tot + other
        else:
            p1, p2 = _halves(p, h)
            t1, t2 = _halves(tot, h)
            k1, _ = _halves(k, h)
            _, q2 = _halves(q, h)
            x = _interleave([kk * jnp.exp2(tt - pp) for kk, tt, pp in zip(k1, t1, p1)],
                            [qq * jnp.exp2(pp) for qq, pp in zip(q2, p2)])
            p = _interleave(p1, [pp + tt for pp, tt in zip(p2, t1)])
            both = [a + b for a, b in zip(t1, t2)]
            tot = _interleave(both, both)
        xs.append(x.astype(BF16))
    return xs, p, tot


def _pair_level_code(n, levels):
    t = lax.broadcasted_iota(jnp.int32, (n, 2 * n), 0)
    s = lax.broadcasted_iota(jnp.int32, (n, 2 * n), 1) & (n - 1)
    x = t ^ s
    code = jnp.where(s == t, 0, -1)
    for i, h in enumerate(levels):
        code = jnp.where((s < t) & (x >= h) & (x < 2 * h), i + 1, code)
    return code


def _block_diag(x):
    w = x.shape[1] // 2
    z = jnp.zeros((x.shape[0], w), x.dtype)
    return jnp.concatenate([jnp.concatenate([x[:, :w], z], axis=1),
                            jnp.concatenate([z, x[:, w:]], axis=1)], axis=0)


def _block_diag_t(x):
    n = x.shape[0]
    w = x.shape[1] // 2
    z = jnp.zeros((w - n, w), x.dtype)
    t0 = jnp.concatenate([x[:, :w], z], axis=0).T
    t1 = jnp.concatenate([z, x[:, w:]], axis=0).T
    return jnp.concatenate([t0, t1], axis=0)


def _pair_grams(xs):
    return [jnp.dot(x, _block_diag_t(x), preferred_element_type=F32) for x in xs]


def _pair_scores(grams, levels, qk, code):
    n = qk.shape[0]
    w = qk.shape[1] // 2
    lane = lax.broadcasted_iota(jnp.int32, (1, 2 * n), 1)
    diag = jnp.where(lane < n, jnp.sum(qk[:, :w], axis=1, keepdims=True),
                     jnp.sum(qk[:, w:], axis=1, keepdims=True))
    pieces = []
    for r0 in range(0, n, 8):
        rows = slice(r0, r0 + 8)
        c = code[rows]
        a = jnp.where(c == 0, diag[rows], 0.0)
        for i, h in enumerate(levels):
            if h < 8 or (r0 & h):
                a = jnp.where(c == i + 1, grams[i][rows], a)
        pieces.append(a)
    return jnp.concatenate(pieces, axis=0)


def _proj(hb, win_ref, g, j):
    col = g * D_MODEL + j * LANE_TILE
    return jnp.dot(hb, win_ref[:, col:col + LANE_TILE], preferred_element_type=F32)


def _project_recurrent(j, hb, win_ref, lb_ref, q_s, k_s, lf_s, v_s):
    qv = _proj(hb, win_ref, 0, j)
    q_s[j] = qv * _sigmoid(qv) * Q_SCALE
    lb = lb_ref[j]
    one_m = 1.0 - lb
    sig = _sigmoid(_proj(hb, win_ref, 1, j))
    lf_s[j] = jnp.log(jnp.maximum(lb, LB_FLOOR) + one_m * sig) * LOG2_E
    k_s[j] = one_m * (1.0 - sig)
    v_s[j] = _proj(hb, win_ref, 2, j).astype(BF16)


def _project_gates(j, hb, win_ref, cw_ref, delayed, ga_s, yb_s, ra_s, rb_s, u_s):
    proj = functools.partial(_proj, hb, win_ref, j=j)
    ga = proj(3)
    ga_s[j] = ga * _sigmoid(ga)
    yield ga
    cg = proj(5)
    yield cg
    hv = proj(6)
    u = cg * hv
    cw = cw_ref[j]
    u1, u2 = delayed(u)
    conv = u2 * cw[0:1, :] + u1 * cw[1:2, :] + u * cw[2:3, :]
    u_s[j] = u
    yield hv
    bg = proj(4)
    conv = conv * bg
    yield bg
    gb = proj(7)
    yb_s[j] = (conv * (gb * _sigmoid(gb))).astype(BF16)
    yield gb
    ra = proj(8)
    ra_s[j] = _sigmoid(ra)
    yield ra
    rb = proj(9)
    rb_s[j] = _sigmoid(rb)
    yield rb


def _output_phase(x, gate, o_s, ga_s, yb_s, ra_s, rb_s, gn_ref, wa_ref, wb_ref, wo_ref, lng_ref, lnb_ref):
    ya = None
    yb = None
    for j in range(N_TILES):
        o = o_s[j]
        parts = []
        for hh in range(HEADS_PER_TILE):
            oh = o[:, hh * HEAD_DIM:(hh + 1) * HEAD_DIM]
            ms = jnp.mean(oh * oh, axis=1, keepdims=True)
            parts.append(oh * lax.rsqrt(ms + RMS_EPS))
        on = jnp.concatenate(parts, axis=1) * gn_ref[j]
        da = jnp.dot((on * ga_s[j]).astype(BF16), wa_ref[j], preferred_element_type=F32)
        db = jnp.dot(yb_s[j], wb_ref[j], preferred_element_type=F32)
        ya = da if ya is None else ya + da
        yb = db if yb is None else yb + db
    out = None
    for j in range(N_TILES):
        lanes = slice(j * LANE_TILE, (j + 1) * LANE_TILE)
        m = ra_s[j] * ya[:, lanes] + rb_s[j] * yb[:, lanes]
        dm = jnp.dot(m.astype(BF16), wo_ref[j], preferred_element_type=F32)
        out = dm if out is None else out + dm
    res = DEEPNORM_ALPHA * x + gate * out
    mu = jnp.mean(res, axis=1, keepdims=True)
    cen = res - mu
    var = jnp.mean(cen * cen, axis=1, keepdims=True)
    return cen * lax.rsqrt(var + LN_EPS) * lng_ref[...] + lnb_ref[...]


def _modulated_input(x, ada):
    shift = ada[:, 0:D_MODEL]
    scale = ada[:, D_MODEL:2 * D_MODEL]
    gate = ada[:, 2 * D_MODEL:3 * D_MODEL]
    return x * (1.0 + scale) + shift, gate


def _prompt_layer_kernel(x_ref, ada_ref, lb_ref, gn_ref, cw_ref, lng_ref, lnb_ref,
                         win_ref, wa_ref, wb_ref, wo_ref,
                         y_ref, sfin_ref, cfin_ref,
                         hb_s, q_s, k_s, lf_s, v_s, ga_s, yb_s, ra_s, rb_s, u_s, o_s, st_s, cprev_s):
    t_blk = pl.program_id(1)
    tb = x_ref.shape[1]

    @pl.when(t_blk == 0)
    def _():
        st_s[...] = jnp.zeros_like(st_s)
        cprev_s[...] = jnp.zeros_like(cprev_s)

    x = x_ref[0]
    h, gate = _modulated_input(x, ada_ref[0])
    hb_s[...] = h.astype(BF16)

    rows_tb = lax.broadcasted_iota(jnp.int32, (tb, 1), 0)

    for j in range(N_TILES):
        _project_recurrent(j, hb_s[...], win_ref.at[0], lb_ref.at[0], q_s, k_s, lf_s, v_s)

    row = lax.broadcasted_iota(jnp.int32, (CHUNK, 1), 0)
    code = _pair_level_code(CHUNK, TREE_LEVELS)

    def gates(j):
        prev = cprev_s[j]

        def delayed(u):
            u1 = jnp.where(rows_tb == 0, prev[7:8, :], _roll_rows(u, 1))
            u2 = jnp.where(rows_tb == 0, prev[6:7, :],
                           jnp.where(rows_tb == 1, prev[7:8, :], _roll_rows(u, 2)))
            return u1, u2

        yield from _project_gates(j, hb_s[...], win_ref.at[0], cw_ref.at[0], delayed,
                                  ga_s, yb_s, ra_s, rb_s, u_s)
        cprev_s[j] = u_s[j, tb - 8:tb, :]

    filler = itertools.chain.from_iterable(gates(j) for j in range(N_TILES))

    def stage_front(c, j):
        rows = pl.ds(c * CHUNK, CHUNK)
        q = q_s[j, rows, :]
        k = k_s[j, rows, :]
        v = v_s[j, rows, :]
        xs, gc, tot = _tree_factors(q, k, lf_s[j, rows, :], TREE_LEVELS, row)
        grams = _pair_grams(xs)
        next(filler, None)
        return rows, j, q, k, v, gc, tot, grams

    def stage_back(rows, j, q, k, v, gc, tot, grams):
        dec = jnp.exp2(tot[0:1, :])
        qg = (q * jnp.exp2(gc)).astype(BF16)
        kd = (k * jnp.exp2(tot - gc)).astype(BF16)
        a = _pair_scores(grams, TREE_LEVELS, q * k, code)
        states = [st_s[j * HEADS_PER_TILE + hh] for hh in range(HEADS_PER_TILE)]
        zero = jnp.zeros((HEAD_DIM, HEAD_DIM), BF16)
        state_bd = jnp.concatenate(
            [jnp.concatenate([states[0].astype(BF16), zero], axis=1),
             jnp.concatenate([zero, states[1].astype(BF16)], axis=1)], axis=0)
        o = jnp.dot(jnp.concatenate([a.astype(BF16), qg], axis=1),
                    jnp.concatenate([_block_diag(v), state_bd], axis=0), preferred_element_type=F32)
        for hh in range(HEADS_PER_TILE):
            lanes = slice(hh * HEAD_DIM, (hh + 1) * HEAD_DIM)
            dec_col = jnp.broadcast_to(dec[:, lanes], (HEAD_DIM, HEAD_DIM)).T
            st_s[j * HEADS_PER_TILE + hh] = states[hh] * dec_col + lax.dot_general(
                kd[:, lanes], v[:, lanes], _TN, preferred_element_type=F32)
        next(filler, None)
        o_s[j, rows, :] = o

    pending = None
    for c in range(tb // CHUNK):
        for j in range(N_TILES):
            front = stage_front(c, j)
            if pending is not None:
                stage_back(*pending)
            pending = front
    stage_back(*pending)
    for _ in filler:
        pass

    y_ref[0] = _output_phase(x, gate, o_s, ga_s, yb_s, ra_s, rb_s, gn_ref.at[0], wa_ref.at[0],
                             wb_ref.at[0], wo_ref.at[0], lng_ref.at[0], lnb_ref.at[0])

    @pl.when(t_blk == pl.num_programs(1) - 1)
    def _():
        sfin_ref[0] = st_s[...]
        for j in range(N_TILES):
            cfin_ref[0, :, j * LANE_TILE:(j + 1) * LANE_TILE] = cprev_s[j, 8 - (CONV_W - 1):8, :]


def _layer_spec(arr, index_of_layer):
    nd = arr.ndim
    return pl.BlockSpec((1,) + arr.shape[1:], lambda *ids: (index_of_layer(*ids),) + (0,) * (nd - 1),
                        pipeline_mode=pl.Buffered(1))


def _tile_scratch(rows):
    f = lambda dt: pltpu.VMEM((N_TILES, rows, LANE_TILE), dt)
    return [f(F32), f(F32), f(F32), f(BF16), f(F32), f(BF16), f(F32), f(F32), f(F32), f(F32)]


def _prompt_layer(layer, x, ada, params, weights):
    bsz, seq, _ = x.shape
    block_t = min(PROMPT_BLOCK_T, seq)
    grid = (bsz, seq // block_t)
    out_shape = (
        jax.ShapeDtypeStruct((bsz, seq, D_MODEL), F32),
        jax.ShapeDtypeStruct((bsz, N_HEADS, HEAD_DIM, HEAD_DIM), F32),
        jax.ShapeDtypeStruct((bsz, CONV_W - 1, D_MODEL), F32),
    )
    at_layer = lambda b, t: layer
    in_specs = ([pl.BlockSpec((1, block_t, D_MODEL), lambda b, t: (b, t, 0)),
                 pl.BlockSpec((1, 1, 3 * D_MODEL), lambda b, t: (b, 0, 0))]
                + [_layer_spec(p, at_layer) for p in params]
                + [_layer_spec(w, at_layer) for w in weights])
    out_specs = (
        pl.BlockSpec((1, block_t, D_MODEL), lambda b, t: (b, t, 0)),
        pl.BlockSpec((1, N_HEADS, HEAD_DIM, HEAD_DIM), lambda b, t: (b, 0, 0, 0)),
        pl.BlockSpec((1, CONV_W - 1, D_MODEL), lambda b, t: (b, 0, 0)),
    )
    scratch = ([pltpu.VMEM((block_t, D_MODEL), BF16)] + _tile_scratch(block_t)
               + [pltpu.VMEM((N_HEADS, HEAD_DIM, HEAD_DIM), F32),
                  pltpu.VMEM((N_TILES, 8, LANE_TILE), F32)])
    return pl.pallas_call(
        _prompt_layer_kernel,
        grid=grid,
        in_specs=in_specs,
        out_specs=out_specs,
        out_shape=out_shape,
        scratch_shapes=scratch,
        compiler_params=pltpu.CompilerParams(
            dimension_semantics=("arbitrary", "arbitrary"),
            vmem_limit_bytes=VMEM_LIMIT_BYTES,
        ),
        name="prompt_layer",
    )(x, ada, *params, *weights)


def _sample_kernel(x_ref, ada_ref, cin_ref, lb_ref, gn_ref, cw_ref, lng_ref, lnb_ref,
                   win_ref, wa_ref, wb_ref, wo_ref, sin_ref,
                   y_ref, sout_ref, cout_ref,
                   xcur_s, hb_s, q_s, k_s, lf_s, v_s, ga_s, yb_s, ra_s, rb_s, u_s, o_s, kdt_s, dct_s):
    layer = pl.program_id(0)
    grp = pl.program_id(1)
    step = pl.program_id(2)
    nseq = SAMPLE_GROUP_SEQ
    n_rows = nseq * SAMPLE_T
    grp_rows = pl.ds(pl.multiple_of(grp * n_rows, n_rows), n_rows)
    tok = lambda t: slice(t * nseq, (t + 1) * nseq)

    @pl.when(step == 0)
    def _():
        @pl.when(layer == 0)
        def _():
            xcur_s[grp_rows, :] = x_ref[...]

        ada = ada_ref[0]
        x = xcur_s[grp_rows, :]
        for t in range(SAMPLE_T):
            h, _ = _modulated_input(x[tok(t)], ada)
            hb_s[tok(t), :] = h.astype(BF16)

        for j in range(N_TILES):
            lanes_j = slice(j * LANE_TILE, (j + 1) * LANE_TILE)
            hist0 = cin_ref[0, 0, :, lanes_j]
            hist1 = cin_ref[0, 1, :, lanes_j]

            def delayed(u):
                return (jnp.concatenate([hist1, u[:(SAMPLE_T - 1) * nseq]], axis=0),
                        jnp.concatenate([hist0, hist1, u[:(SAMPLE_T - 2) * nseq]], axis=0))

            _project_recurrent(j, hb_s[...], win_ref.at[0], lb_ref.at[0], q_s, k_s, lf_s, v_s)
            for _ in _project_gates(j, hb_s[...], win_ref.at[0], cw_ref.at[0], delayed,
                                    ga_s, yb_s, ra_s, rb_s, u_s):
                pass

            q = [q_s[j, tok(t), :] for t in range(SAMPLE_T)]
            k = [k_s[j, tok(t), :] for t in range(SAMPLE_T)]
            v = [v_s[j, tok(t), :].astype(F32) for t in range(SAMPLE_T)]
            gc = [lf_s[j, tok(0), :]]
            for t in range(1, SAMPLE_T):
                gc.append(gc[-1] + lf_s[j, tok(t), :])
            tot = gc[-1]
            for t in range(SAMPLE_T):
                o_heads = [jnp.zeros((nseq, HEAD_DIM), F32) for _ in range(HEADS_PER_TILE)]
                for s in range(t + 1):
                    w = q[t] * k[s]
                    if s < t:
                        w = w * jnp.exp2(gc[t] - gc[s])
                    for hh in range(HEADS_PER_TILE):
                        lanes = slice(hh * HEAD_DIM, (hh + 1) * HEAD_DIM)
                        a = jnp.sum(w[:, lanes], axis=1, keepdims=True)
                        o_heads[hh] = o_heads[hh] + a * v[s][:, lanes]
                o_s[j, tok(t), :] = jnp.concatenate(o_heads, axis=1)
            for t in range(SAMPLE_T):
                q_s[j, tok(t), :] = q[t] * jnp.exp2(gc[t])
                k_s[j, tok(t), :] = k[t] * jnp.exp2(tot - gc[t])
                lf_s[j, tok(t), :] = jnp.exp2(tot)
            for hh in range(HEADS_PER_TILE):
                lanes = slice(hh * HEAD_DIM, (hh + 1) * HEAD_DIM)
                hd = j * HEADS_PER_TILE + hh
                kdt_s[hd] = k_s[j, :, lanes].T.astype(BF16)
                dct_s[hd] = lf_s[j, :, lanes].T

    seq0 = step * SAMPLE_SEQ_PER_STEP
    piece0 = pl.multiple_of((seq0 // 8) * 8, 8)
    lane_id = lax.broadcasted_iota(jnp.int32, (1, n_rows), 1)
    sub_id = lax.broadcasted_iota(jnp.int32, (8 * SAMPLE_T, 1), 0) & 7

    def seq_state(j, carry):
        for hh in range(HEADS_PER_TILE):
            lanes = slice(hh * HEAD_DIM, (hh + 1) * HEAD_DIM)
            hd = j * HEADS_PER_TILE + hh
            qg = jnp.concatenate([q_s[j, pl.ds(t * nseq + piece0, 8), lanes] for t in range(SAMPLE_T)],
                                 axis=0).astype(BF16)
            vblk = v_s[j, :, lanes]
            kdt = kdt_s[hd]
            dct = dct_s[hd]
            o_acc = jnp.zeros((8 * SAMPLE_T, HEAD_DIM), F32)
            for s in range(SAMPLE_SEQ_PER_STEP):
                seq = seq0 + s
                st = sin_ref[0, s, hd]
                res = jnp.dot(qg, st.astype(BF16), preferred_element_type=F32)
                o_acc = o_acc + jnp.where(sub_id == (seq & 7), res, 0.0)
                mine = (lane_id & (nseq - 1)) == seq
                upd = jnp.dot(jnp.where(mine, kdt, jnp.zeros_like(kdt)), vblk, preferred_element_type=F32)
                dcol = jnp.sum(jnp.where(lane_id[:, :nseq] == seq, dct[:, :nseq], 0.0), axis=1, keepdims=True)
                sout_ref[0, s, hd] = st * dcol + upd
            for t in range(SAMPLE_T):
                rows = pl.ds(t * nseq + piece0, 8)
                o_s[j, rows, lanes] = o_s[j, rows, lanes] + o_acc[8 * t:8 * (t + 1)]
        return carry

    lax.fori_loop(0, N_TILES, seq_state, 0)

    @pl.when(step == pl.num_programs(2) - 1)
    def _():
        ada = ada_ref[0]
        gate = jnp.concatenate([ada[:, 2 * D_MODEL:3 * D_MODEL]] * SAMPLE_T, axis=0)
        y = _output_phase(xcur_s[grp_rows, :], gate, o_s, ga_s, yb_s, ra_s, rb_s, gn_ref.at[0],
                          wa_ref.at[0], wb_ref.at[0], wo_ref.at[0], lng_ref.at[0], lnb_ref.at[0])
        y_ref[0] = y
        xcur_s[grp_rows, :] = y
        for j in range(N_TILES):
            for i in range(CONV_W - 1):
                cout_ref[0, i, :, j * LANE_TILE:(j + 1) * LANE_TILE] = u_s[j, tok(SAMPLE_T - (CONV_W - 1) + i), :]


def _sample_trunk(x, ada, conv_in, params, weights, state):
    n_rows_all = x.shape[0]
    n_seq = state.shape[1]
    n_grp = n_seq // SAMPLE_GROUP_SEQ
    grp_rows = SAMPLE_GROUP_SEQ * SAMPLE_T
    steps = SAMPLE_GROUP_SEQ // SAMPLE_SEQ_PER_STEP
    grid = (DEPTH, n_grp, steps)
    state_blk = (1, SAMPLE_SEQ_PER_STEP, N_HEADS, HEAD_DIM, HEAD_DIM)
    state_idx = lambda l, g, s: (l, g * steps + s, 0, 0, 0)
    conv_blk = (1, CONV_W - 1, SAMPLE_GROUP_SEQ, D_MODEL)
    conv_idx = lambda l, g, s: (l, 0, g, 0)
    at_layer = lambda l, g, s: l
    out_shape = (
        jax.ShapeDtypeStruct((DEPTH, n_rows_all, D_MODEL), F32),
        jax.ShapeDtypeStruct(state.shape, F32),
        jax.ShapeDtypeStruct(conv_in.shape, F32),
    )
    in_specs = ([pl.BlockSpec((grp_rows, D_MODEL), lambda l, g, s: (g, 0)),
                 pl.BlockSpec((1, SAMPLE_GROUP_SEQ, 3 * D_MODEL), lambda l, g, s: (l, g, 0)),
                 pl.BlockSpec(conv_blk, conv_idx)]
                + [_layer_spec(p, at_layer) for p in params]
                + [_layer_spec(w, at_layer) for w in weights]
                + [pl.BlockSpec(state_blk, state_idx)])
    out_specs = (
        pl.BlockSpec((1, grp_rows, D_MODEL), lambda l, g, s: (l, g, 0)),
        pl.BlockSpec(state_blk, state_idx),
        pl.BlockSpec(conv_blk, conv_idx),
    )
    scratch = ([pltpu.VMEM((n_rows_all, D_MODEL), F32), pltpu.VMEM((grp_rows, D_MODEL), BF16)]
               + _tile_scratch(grp_rows)
               + [pltpu.VMEM((N_HEADS, HEAD_DIM, grp_rows), BF16),
                  pltpu.VMEM((N_HEADS, HEAD_DIM, grp_rows), F32)])
    return pl.pallas_call(
        _sample_kernel,
        grid=grid,
        in_specs=in_specs,
        out_specs=out_specs,
        out_shape=out_shape,
        scratch_shapes=scratch,
        compiler_params=pltpu.CompilerParams(
            dimension_semantics=("arbitrary", "arbitrary", "arbitrary"),
            vmem_limit_bytes=VMEM_LIMIT_BYTES),
        name="sample_trunk",
    )(x, ada, conv_in, *params, *weights, state)


def _prep_kernel(c_ref, wada_ref, bada_ref, lbp_ref, ada_ref, lbs_ref):
    c = c_ref[...]
    sc = (c * _sigmoid(c)).astype(BF16)
    ada_ref[0] = jnp.dot(sc, wada_ref[0].astype(BF16), preferred_element_type=F32) + bada_ref[0]
    p = [lbp_ref[i:i + 1, :] for i in range(DEPTH)]
    mx = functools.reduce(jnp.maximum, p)
    e = [jnp.exp(pi - mx) for pi in p]
    tot = functools.reduce(lambda a, b: a + b, e)
    acc = jnp.zeros_like(mx)
    lbs_ref[0:1, :] = acc
    for i in range(1, DEPTH):
        acc = acc + e[i] / tot
        lbs_ref[i:i + 1, :] = acc


def _prep(c_all, w_ada, b_ada, lb_param):
    n = c_all.shape[0]
    return pl.pallas_call(
        _prep_kernel,
        grid=(DEPTH,),
        in_specs=[
            pl.BlockSpec((n, D_MODEL), lambda l: (0, 0)),
            pl.BlockSpec((1, D_MODEL, 3 * D_MODEL), lambda l: (l, 0, 0)),
            pl.BlockSpec((1, 1, 3 * D_MODEL), lambda l: (l, 0, 0)),
            pl.BlockSpec((DEPTH, D_MODEL), lambda l: (0, 0)),
        ],
        out_specs=(
            pl.BlockSpec((1, n, 3 * D_MODEL), lambda l: (l, 0, 0)),
            pl.BlockSpec((DEPTH, D_MODEL), lambda l: (0, 0)),
        ),
        out_shape=(
            jax.ShapeDtypeStruct((DEPTH, n, 3 * D_MODEL), F32),
            jax.ShapeDtypeStruct((DEPTH, D_MODEL), F32),
        ),
        compiler_params=pltpu.CompilerParams(dimension_semantics=("arbitrary",)),
        name="conditioning",
    )(c_all, w_ada, b_ada.reshape(DEPTH, 1, 3 * D_MODEL), lb_param)


def _tiled(p):
    return p.reshape(DEPTH, p.shape[1], N_TILES, LANE_TILE).transpose(0, 2, 1, 3)


def kernel(x_prompt, x_sample, state_hgrn, state_conv, c_prompt, c_sample, w_ada, b_ada, w_in, hgrn_lb,
           hgrn_gnorm, conv_w, w_a_out, w_b_out, w_o, ln_g, ln_b):
    n_prompt = x_prompt.shape[0]
    n_seq, t_dec, _ = x_sample.shape
    assert t_dec == SAMPLE_T and n_seq % SAMPLE_GROUP_SEQ == 0 and x_prompt.shape[1] % CHUNK == 0
    n_grp = n_seq // SAMPLE_GROUP_SEQ

    c_all = jnp.concatenate([c_prompt, c_sample], axis=0)
    ada_all, lbs = _prep(c_all, w_ada, b_ada, hgrn_lb)

    params = (_tiled(lbs[:, None, :]), _tiled(hgrn_gnorm[:, None, :]), _tiled(conv_w),
              ln_g[:, None, :], ln_b[:, None, :])
    weights = (w_in.astype(BF16),
               w_a_out.astype(BF16).reshape(DEPTH, N_TILES, LANE_TILE, D_MODEL),
               w_b_out.astype(BF16).reshape(DEPTH, N_TILES, LANE_TILE, D_MODEL),
               w_o.astype(BF16).reshape(DEPTH, N_TILES, LANE_TILE, D_MODEL))

    xp = x_prompt
    hgrn_p, conv_p = [], []
    for l in range(DEPTH):
        ada_p = ada_all[l, :n_prompt].reshape(n_prompt, 1, 3 * D_MODEL)
        xp, s_new, c_new = _prompt_layer(l, xp, ada_p, params, weights)
        hgrn_p.append(s_new)
        conv_p.append(c_new)

    xs = (x_sample.reshape(n_grp, SAMPLE_GROUP_SEQ, t_dec, D_MODEL).transpose(0, 2, 1, 3)
          .reshape(n_seq * t_dec, D_MODEL))
    ys, hgrn_s, conv_s = _sample_trunk(xs, ada_all[:, n_prompt:], state_conv.transpose(0, 2, 1, 3),
                                       params, weights, state_hgrn)
    ys = (ys[DEPTH - 1].reshape(n_grp, t_dec, SAMPLE_GROUP_SEQ, D_MODEL).transpose(0, 2, 1, 3)
          .reshape(n_seq, t_dec, D_MODEL))

    return (xp, ys, jnp.stack(hgrn_p), jnp.stack(conv_p), hgrn_s, conv_s.transpose(0, 2, 1, 3))
```

```python
import functools
import itertools

import jax
import jax.numpy as jnp
from jax import lax
from jax.experimental import pallas as pl
from jax.experimental.pallas import tpu as pltpu

F32 = jnp.float32
BF16 = jnp.bfloat16

D_MODEL = 1024
N_HEADS = 8
HEAD_DIM = 128
N_GROUPS = 10
LANE_TILE = 256
HEADS_PER_TILE = LANE_TILE // HEAD_DIM
N_TILES = D_MODEL // LANE_TILE
CONV_W = 3
CHUNK = 64
DEPTH = 4
DEEPNORM_ALPHA = (2 * DEPTH) ** 0.25
LN_EPS = 1e-5
RMS_EPS = 1e-6
LB_FLOOR = 1e-30
Q_SCALE = HEAD_DIM ** -0.5
LOG2_E = 1.4426950408889634
TREE_LEVELS = (1, 2, 4, 8, 16, 32)
PROMPT_BLOCK_T = 512
SAMPLE_T = 4
SAMPLE_GROUP_SEQ = 32
SAMPLE_SEQ_PER_STEP = 8
VMEM_LIMIT_BYTES = 64 * 1024 * 1024 - 256 * 1024

_TN = (((0,), (0,)), ((), ()))


def _sigmoid(x):
    return 1.0 / (1.0 + jnp.exp2(x * (-LOG2_E)))


def _roll_rows(x, shift):
    return pltpu.roll(x, shift % x.shape[0], axis=0)


def _roll_in_groups(x, shift):
    n, w = x.shape
    return pltpu.roll(x.reshape(n // 8, 8, w), shift % 8, axis=1).reshape(n, w)


def _halves(x, h):
    n = x.shape[0]
    first = [x[b:b + h] for b in range(0, n, 2 * h)]
    second = [x[b + h:b + 2 * h] for b in range(0, n, 2 * h)]
    return first, second


def _interleave(first, second):
    out = []
    for a, b in zip(first, second):
        out += [a, b]
    return jnp.concatenate(out, axis=0)


def _tree_factors(q, k, lf, levels, row):
    p = lf
    tot = lf
    xs = []
    for h in levels:
        if h < 8:
            second = (row & h) != 0
            other = _roll_in_groups(tot, h)
            if h != 4:
                other = jnp.where(second, other, _roll_in_groups(tot, -h))
            x = jnp.where(second, q, k) * jnp.exp2(jnp.where(second, p, tot - p))
            p = p + jnp.where(second, other, 0.0)
            tot = tot + other
        else:
            p1, p2 = _halves(p, h)
            t1, t2 = _halves(tot, h)
            k1, _ = _halves(k, h)
            _, q2 = _halves(q, h)
            x = _interleave([kk * jnp.exp2(tt - pp) for kk, tt, pp in zip(k1, t1, p1)],
                            [qq * jnp.exp2(pp) for qq, pp in zip(q2, p2)])
            p = _interleave(p1, [pp + tt for pp, tt in zip(p2, t1)])
            both = [a + b for a, b in zip(t1, t2)]
            tot = _interleave(both, both)
        xs.append(x.astype(BF16))
    return xs, p, tot


def _pair_level_code(n, levels):
    t = lax.broadcasted_iota(jnp.int32, (n, 2 * n), 0)
    s = lax.broadcasted_iota(jnp.int32, (n, 2 * n), 1) & (n - 1)
    x = t ^ s
    code = jnp.where(s == t, 0, -1)
    for i, h in enumerate(levels):
        code = jnp.where((s < t) & (x >= h) & (x < 2 * h), i + 1, code)
    return code


def _block_diag(x):
    w = x.shape[1] // 2
    z = jnp.zeros((x.shape[0], w), x.dtype)
    return jnp.concatenate([jnp.concatenate([x[:, :w], z], axis=1),
                            jnp.concatenate([z, x[:, w:]], axis=1)], axis=0)


def _block_diag_t(x):
    n = x.shape[0]
    w = x.shape[1] // 2
    z = jnp.zeros((w - n, w), x.dtype)
    t0 = jnp.concatenate([x[:, :w], z], axis=0).T
    t1 = jnp.concatenate([z, x[:, w:]], axis=0).T
    return jnp.concatenate([t0, t1], axis=0)


def _pair_grams(xs):
    return [jnp.dot(x, _block_diag_t(x), preferred_element_type=F32) for x in xs]


def _pair_scores(grams, levels, qk, code):
    n = qk.shape[0]
    w = qk.shape[1] // 2
    lane = lax.broadcasted_iota(jnp.int32, (1, 2 * n), 1)
    diag = jnp.where(lane < n, jnp.sum(qk[:, :w], axis=1, keepdims=True),
                     jnp.sum(qk[:, w:], axis=1, keepdims=True))
    pieces = []
    for r0 in range(0, n, 8):
        rows = slice(r0, r0 + 8)
        c = code[rows]
        a = jnp.where(c == 0, diag[rows], 0.0)
        for i, h in enumerate(levels):
            if h < 8 or (r0 & h):
                a = jnp.where(c == i + 1, grams[i][rows], a)
        pieces.append(a)
    return jnp.concatenate(pieces, axis=0)


def _proj(hb, win_ref, g, j):
    col = g * D_MODEL + j * LANE_TILE
    return jnp.dot(hb, win_ref[:, col:col + LANE_TILE], preferred_element_type=F32)


def _project_recurrent(j, hb, win_ref, lb_ref, q_s, k_s, lf_s, v_s):
    qv = _proj(hb, win_ref, 0, j)
    q_s[j] = qv * _sigmoid(qv) * Q_SCALE
    lb = lb_ref[j]
    one_m = 1.0 - lb
    sig = _sigmoid(_proj(hb, win_ref, 1, j))
    lf_s[j] = jnp.log(jnp.maximum(lb, LB_FLOOR) + one_m * sig) * LOG2_E
    k_s[j] = one_m * (1.0 - sig)
    v_s[j] = _proj(hb, win_ref, 2, j).astype(BF16)


def _project_gates(j, hb, win_ref, cw_ref, delayed, ga_s, yb_s, ra_s, rb_s, u_s):
    proj = functools.partial(_proj, hb, win_ref, j=j)
    ga = proj(3)
    ga_s[j] = ga * _sigmoid(ga)
    yield ga
    cg = proj(5)
    yield cg
    hv = proj(6)
    u = cg * hv
    cw = cw_ref[j]
    u1, u2 = delayed(u)
    conv = u2 * cw[0:1, :] + u1 * cw[1:2, :] + u * cw[2:3, :]
    u_s[j] = u
    yield hv
    bg = proj(4)
    conv = conv * bg
    yield bg
    gb = proj(7)
    yb_s[j] = (conv * (gb * _sigmoid(gb))).astype(BF16)
    yield gb
    ra = proj(8)
    ra_s[j] = _sigmoid(ra)
    yield ra
    rb = proj(9)
    rb_s[j] = _sigmoid(rb)
    yield rb


def _output_phase(x, gate, o_s, ga_s, yb_s, ra_s, rb_s, gn_ref, wa_ref, wb_ref, wo_ref, lng_ref, lnb_ref,
                  row_blocks=1):
    if row_blocks > 1:
        step = x.shape[0] // row_blocks
        outs = []
        for r0 in range(0, x.shape[0], step):
            rows = slice(r0, r0 + step)
            view = lambda ref: ref.at[:, rows, :]
            outs.append(_output_phase(x[rows], gate if gate.shape[0] == 1 else gate[rows],
                                      view(o_s), view(ga_s), view(yb_s), view(ra_s), view(rb_s),
                                      gn_ref, wa_ref, wb_ref, wo_ref, lng_ref, lnb_ref))
        return jnp.concatenate(outs, axis=0)
    ya = None
    yb = None
    for j in range(N_TILES):
        o = o_s[j]
        parts = []
        for hh in range(HEADS_PER_TILE):
            oh = o[:, hh * HEAD_DIM:(hh + 1) * HEAD_DIM]
            ms = jnp.mean(oh * oh, axis=1, keepdims=True)
            parts.append(oh * lax.rsqrt(ms + RMS_EPS))
        on = jnp.concatenate(parts, axis=1) * gn_ref[j]
        da = jnp.dot((on * ga_s[j]).astype(BF16), wa_ref[j], preferred_element_type=F32)
        db = jnp.dot(yb_s[j], wb_ref[j], preferred_element_type=F32)
        ya = da if ya is None else ya + da
        yb = db if yb is None else yb + db
    out = None
    for j in range(N_TILES):
        lanes = slice(j * LANE_TILE, (j + 1) * LANE_TILE)
        m = ra_s[j] * ya[:, lanes] + rb_s[j] * yb[:, lanes]
        dm = jnp.dot(m.astype(BF16), wo_ref[j], preferred_element_type=F32)
        out = dm if out is None else out + dm
    res = DEEPNORM_ALPHA * x + gate * out
    mu = jnp.mean(res, axis=1, keepdims=True)
    cen = res - mu
    var = jnp.mean(cen * cen, axis=1, keepdims=True)
    return cen * lax.rsqrt(var + LN_EPS) * lng_ref[...] + lnb_ref[...]


def _modulated_input(x, ada):
    shift = ada[:, 0:D_MODEL]
    scale = ada[:, D_MODEL:2 * D_MODEL]
    gate = ada[:, 2 * D_MODEL:3 * D_MODEL]
    return x * (1.0 + scale) + shift, gate


def _prompt_layer_kernel(x_ref, ada_ref, lb_ref, gn_ref, cw_ref, lng_ref, lnb_ref,
                         win_ref, wa_ref, wb_ref, wo_ref,
                         y_ref, sfin_ref, cfin_ref,
                         hb_s, q_s, k_s, lf_s, v_s, ga_s, yb_s, ra_s, rb_s, u_s, o_s, st_s, cprev_s):
    t_blk = pl.program_id(1)
    tb = x_ref.shape[1]

    @pl.when(t_blk == 0)
    def _():
        st_s[...] = jnp.zeros_like(st_s)
        cprev_s[...] = jnp.zeros_like(cprev_s)

    x = x_ref[0]
    h, gate = _modulated_input(x, ada_ref[0])
    hb_s[...] = h.astype(BF16)

    rows_tb = lax.broadcasted_iota(jnp.int32, (tb, 1), 0)

    for j in range(N_TILES):
        _project_recurrent(j, hb_s[...], win_ref.at[0], lb_ref.at[0], q_s, k_s, lf_s, v_s)

    row = lax.broadcasted_iota(jnp.int32, (CHUNK, 1), 0)
    code = _pair_level_code(CHUNK, TREE_LEVELS)

    def gates(j):
        prev = cprev_s[j]

        def delayed(u):
            u1 = jnp.where(rows_tb == 0, prev[7:8, :], _roll_rows(u, 1))
            u2 = jnp.where(rows_tb == 0, prev[6:7, :],
                           jnp.where(rows_tb == 1, prev[7:8, :], _roll_rows(u, 2)))
            return u1, u2

        yield from _project_gates(j, hb_s[...], win_ref.at[0], cw_ref.at[0], delayed,
                                  ga_s, yb_s, ra_s, rb_s, u_s)
        cprev_s[j] = u_s[j, tb - 8:tb, :]

    filler = itertools.chain.from_iterable(gates(j) for j in range(N_TILES))

    def stage_front(c, j):
        rows = pl.ds(c * CHUNK, CHUNK)
        q = q_s[j, rows, :]
        k = k_s[j, rows, :]
        v = v_s[j, rows, :]
        xs, gc, tot = _tree_factors(q, k, lf_s[j, rows, :], TREE_LEVELS, row)
        grams = _pair_grams(xs)
        next(filler, None)
        return rows, j, q, k, v, gc, tot, grams

    def stage_back(rows, j, q, k, v, gc, tot, grams):
        dec = jnp.exp2(tot[0:1, :])
        qg = (q * jnp.exp2(gc)).astype(BF16)
        kd = (k * jnp.exp2(tot - gc)).astype(BF16)
        a = _pair_scores(grams, TREE_LEVELS, q * k, code)
        states = [st_s[j * HEADS_PER_TILE + hh] for hh in range(HEADS_PER_TILE)]
        zero = jnp.zeros((HEAD_DIM, HEAD_DIM), BF16)
        state_bd = jnp.concatenate(
            [jnp.concatenate([states[0].astype(BF16), zero], axis=1),
             jnp.concatenate([zero, states[1].astype(BF16)], axis=1)], axis=0)
        o = jnp.dot(jnp.concatenate([a.astype(BF16), qg], axis=1),
                    jnp.concatenate([_block_diag(v), state_bd], axis=0), preferred_element_type=F32)
        for hh in range(HEADS_PER_TILE):
            lanes = slice(hh * HEAD_DIM, (hh + 1) * HEAD_DIM)
            dec_col = jnp.broadcast_to(dec[:, lanes], (HEAD_DIM, HEAD_DIM)).T
            st_s[j * HEADS_PER_TILE + hh] = states[hh] * dec_col + lax.dot_general(
                kd[:, lanes], v[:, lanes], _TN, preferred_element_type=F32)
        next(filler, None)
        o_s[j, rows, :] = o

    pending = None
    for c in range(tb // CHUNK):
        for j in range(N_TILES):
            front = stage_front(c, j)
            if pending is not None:
                stage_back(*pending)
            pending = front
    stage_back(*pending)
    for _ in filler:
        pass

    y_ref[0] = _output_phase(x, gate, o_s, ga_s, yb_s, ra_s, rb_s, gn_ref.at[0], wa_ref.at[0],
                             wb_ref.at[0], wo_ref.at[0], lng_ref.at[0], lnb_ref.at[0], row_blocks=2)

    @pl.when(t_blk == pl.num_programs(1) - 1)
    def _():
        sfin_ref[0] = st_s[...]
        for j in range(N_TILES):
            cfin_ref[0, :, j * LANE_TILE:(j + 1) * LANE_TILE] = cprev_s[j, 8 - (CONV_W - 1):8, :]


def _layer_spec(arr, index_of_layer):
    nd = arr.ndim
    return pl.BlockSpec((1,) + arr.shape[1:], lambda *ids: (index_of_layer(*ids),) + (0,) * (nd - 1),
                        pipeline_mode=pl.Buffered(1))


def _tile_scratch(rows):
    f = lambda dt: pltpu.VMEM((N_TILES, rows, LANE_TILE), dt)
    return [f(F32), f(F32), f(F32), f(BF16), f(F32), f(BF16), f(F32), f(F32), f(F32), f(F32)]


def _prompt_layer(layer, x, ada, params, weights):
    bsz, seq, _ = x.shape
    block_t = min(PROMPT_BLOCK_T, seq)
    grid = (bsz, seq // block_t)
    out_shape = (
        jax.ShapeDtypeStruct((bsz, seq, D_MODEL), F32),
        jax.ShapeDtypeStruct((bsz, N_HEADS, HEAD_DIM, HEAD_DIM), F32),
        jax.ShapeDtypeStruct((bsz, CONV_W - 1, D_MODEL), F32),
    )
    at_layer = lambda b, t: layer
    in_specs = ([pl.BlockSpec((1, block_t, D_MODEL), lambda b, t: (b, t, 0)),
                 pl.BlockSpec((1, 1, 3 * D_MODEL), lambda b, t: (b, 0, 0))]
                + [_layer_spec(p, at_layer) for p in params]
                + [_layer_spec(w, at_layer) for w in weights])
    out_specs = (
        pl.BlockSpec((1, block_t, D_MODEL), lambda b, t: (b, t, 0)),
        pl.BlockSpec((1, N_HEADS, HEAD_DIM, HEAD_DIM), lambda b, t: (b, 0, 0, 0)),
        pl.BlockSpec((1, CONV_W - 1, D_MODEL), lambda b, t: (b, 0, 0)),
    )
    scratch = ([pltpu.VMEM((block_t, D_MODEL), BF16)] + _tile_scratch(block_t)
               + [pltpu.VMEM((N_HEADS, HEAD_DIM, HEAD_DIM), F32),
                  pltpu.VMEM((N_TILES, 8, LANE_TILE), F32)])
    return pl.pallas_call(
        _prompt_layer_kernel,
        grid=grid,
        in_specs=in_specs,
        out_specs=out_specs,
        out_shape=out_shape,
        scratch_shapes=scratch,
        compiler_params=pltpu.CompilerParams(
            dimension_semantics=("arbitrary", "arbitrary"),
            vmem_limit_bytes=VMEM_LIMIT_BYTES,
        ),
        name="prompt_layer",
    )(x, ada, *params, *weights)


def _sample_kernel(x_ref, ada_ref, cin_ref, lb_ref, gn_ref, cw_ref, lng_ref, lnb_ref,
                   win_ref, wa_ref, wb_ref, wo_ref, sin_ref,
                   y_ref, sout_ref, cout_ref,
                   xcur_s, hb_s, q_s, k_s, lf_s, v_s, ga_s, yb_s, ra_s, rb_s, u_s, o_s, kdt_s, dct_s):
    layer = pl.program_id(0)
    grp = pl.program_id(1)
    step = pl.program_id(2)
    nseq = SAMPLE_GROUP_SEQ
    n_rows = nseq * SAMPLE_T
    grp_rows = pl.ds(pl.multiple_of(grp * n_rows, n_rows), n_rows)
    tok = lambda t: slice(t * nseq, (t + 1) * nseq)

    @pl.when(step == 0)
    def _():
        @pl.when(layer == 0)
        def _():
            xcur_s[grp_rows, :] = x_ref[...]

        ada = ada_ref[0]
        x = xcur_s[grp_rows, :]
        for t in range(SAMPLE_T):
            h, _ = _modulated_input(x[tok(t)], ada)
            hb_s[tok(t), :] = h.astype(BF16)

        for j in range(N_TILES):
            lanes_j = slice(j * LANE_TILE, (j + 1) * LANE_TILE)
            hist0 = cin_ref[0, 0, :, lanes_j]
            hist1 = cin_ref[0, 1, :, lanes_j]

            def delayed(u):
                return (jnp.concatenate([hist1, u[:(SAMPLE_T - 1) * nseq]], axis=0),
                        jnp.concatenate([hist0, hist1, u[:(SAMPLE_T - 2) * nseq]], axis=0))

            _project_recurrent(j, hb_s[...], win_ref.at[0], lb_ref.at[0], q_s, k_s, lf_s, v_s)
            for _ in _project_gates(j, hb_s[...], win_ref.at[0], cw_ref.at[0], delayed,
                                    ga_s, yb_s, ra_s, rb_s, u_s):
                pass

            q = [q_s[j, tok(t), :] for t in range(SAMPLE_T)]
            k = [k_s[j, tok(t), :] for t in range(SAMPLE_T)]
            v = [v_s[j, tok(t), :].astype(F32) for t in range(SAMPLE_T)]
            gc = [lf_s[j, tok(0), :]]
            for t in range(1, SAMPLE_T):
                gc.append(gc[-1] + lf_s[j, tok(t), :])
            tot = gc[-1]
            for t in range(SAMPLE_T):
                o_heads = [jnp.zeros((nseq, HEAD_DIM), F32) for _ in range(HEADS_PER_TILE)]
                for s in range(t + 1):
                    w = q[t] * k[s]
                    if s < t:
                        w = w * jnp.exp2(gc[t] - gc[s])
                    for hh in range(HEADS_PER_TILE):
                        lanes = slice(hh * HEAD_DIM, (hh + 1) * HEAD_DIM)
                        a = jnp.sum(w[:, lanes], axis=1, keepdims=True)
                        o_heads[hh] = o_heads[hh] + a * v[s][:, lanes]
                o_s[j, tok(t), :] = jnp.concatenate(o_heads, axis=1)
            for t in range(SAMPLE_T):
                q_s[j, tok(t), :] = q[t] * jnp.exp2(gc[t])
                k_s[j, tok(t), :] = k[t] * jnp.exp2(tot - gc[t])
                lf_s[j, tok(t), :] = jnp.exp2(tot)
            for hh in range(HEADS_PER_TILE):
                lanes = slice(hh * HEAD_DIM, (hh + 1) * HEAD_DIM)
                hd = j * HEADS_PER_TILE + hh
                kdt_s[hd] = k_s[j, :, lanes].T.astype(BF16)
                dct_s[hd] = lf_s[j, :, lanes].T

    seq0 = step * SAMPLE_SEQ_PER_STEP
    piece0 = pl.multiple_of((seq0 // 8) * 8, 8)
    lane_id = lax.broadcasted_iota(jnp.int32, (1, n_rows), 1)
    sub_id = lax.broadcasted_iota(jnp.int32, (8 * SAMPLE_T, 1), 0) & 7

    def seq_state(j, carry):
        for hh in range(HEADS_PER_TILE):
            lanes = slice(hh * HEAD_DIM, (hh + 1) * HEAD_DIM)
            hd = j * HEADS_PER_TILE + hh
            qg = jnp.concatenate([q_s[j, pl.ds(t * nseq + piece0, 8), lanes] for t in range(SAMPLE_T)],
                                 axis=0).astype(BF16)
            vblk = v_s[j, :, lanes]
            kdt = kdt_s[hd]
            dct = dct_s[hd]
            o_acc = jnp.zeros((8 * SAMPLE_T, HEAD_DIM), F32)
            for s in range(SAMPLE_SEQ_PER_STEP):
                seq = seq0 + s
                st = sin_ref[0, s, hd]
                res = jnp.dot(qg, st.astype(BF16), preferred_element_type=F32)
                o_acc = o_acc + jnp.where(sub_id == (seq & 7), res, 0.0)
                mine = (lane_id & (nseq - 1)) == seq
                upd = jnp.dot(jnp.where(mine, kdt, jnp.zeros_like(kdt)), vblk, preferred_element_type=F32)
                dcol = jnp.sum(jnp.where(lane_id[:, :nseq] == seq, dct[:, :nseq], 0.0), axis=1, keepdims=True)
                sout_ref[0, s, hd] = st * dcol + upd
            for t in range(SAMPLE_T):
                rows = pl.ds(t * nseq + piece0, 8)
                o_s[j, rows, lanes] = o_s[j, rows, lanes] + o_acc[8 * t:8 * (t + 1)]
        return carry

    lax.fori_loop(0, N_TILES, seq_state, 0)

    @pl.when(step == pl.num_programs(2) - 1)
    def _():
        ada = ada_ref[0]
        gate = jnp.concatenate([ada[:, 2 * D_MODEL:3 * D_MODEL]] * SAMPLE_T, axis=0)
        y = _output_phase(xcur_s[grp_rows, :], gate, o_s, ga_s, yb_s, ra_s, rb_s, gn_ref.at[0],
                          wa_ref.at[0], wb_ref.at[0], wo_ref.at[0], lng_ref.at[0], lnb_ref.at[0])
        y_ref[0] = y
        xcur_s[grp_rows, :] = y
        for j in range(N_TILES):
            for i in range(CONV_W - 1):
                cout_ref[0, i, :, j * LANE_TILE:(j + 1) * LANE_TILE] = u_s[j, tok(SAMPLE_T - (CONV_W - 1) + i), :]


def _sample_trunk(x, ada, conv_in, params, weights, state):
    n_rows_all = x.shape[0]
    n_seq = state.shape[1]
    n_grp = n_seq // SAMPLE_GROUP_SEQ
    grp_rows = SAMPLE_GROUP_SEQ * SAMPLE_T
    steps = SAMPLE_GROUP_SEQ // SAMPLE_SEQ_PER_STEP
    grid = (DEPTH, n_grp, steps)
    state_blk = (1, SAMPLE_SEQ_PER_STEP, N_HEADS, HEAD_DIM, HEAD_DIM)
    state_idx = lambda l, g, s: (l, g * steps + s, 0, 0, 0)
    conv_blk = (1, CONV_W - 1, SAMPLE_GROUP_SEQ, D_MODEL)
    conv_idx = lambda l, g, s: (l, 0, g, 0)
    at_layer = lambda l, g, s: l
    out_shape = (
        jax.ShapeDtypeStruct((DEPTH, n_rows_all, D_MODEL), F32),
        jax.ShapeDtypeStruct(state.shape, F32),
        jax.ShapeDtypeStruct(conv_in.shape, F32),
    )
    in_specs = ([pl.BlockSpec((grp_rows, D_MODEL), lambda l, g, s: (g, 0)),
                 pl.BlockSpec((1, SAMPLE_GROUP_SEQ, 3 * D_MODEL), lambda l, g, s: (l, g, 0)),
                 pl.BlockSpec(conv_blk, conv_idx)]
                + [_layer_spec(p, at_layer) for p in params]
                + [_layer_spec(w, at_layer) for w in weights]
                + [pl.BlockSpec(state_blk, state_idx)])
    out_specs = (
        pl.BlockSpec((1, grp_rows, D_MODEL), lambda l, g, s: (l, g, 0)),
        pl.BlockSpec(state_blk, state_idx),
        pl.BlockSpec(conv_blk, conv_idx),
    )
    scratch = ([pltpu.VMEM((n_rows_all, D_MODEL), F32), pltpu.VMEM((grp_rows, D_MODEL), BF16)]
               + _tile_scratch(grp_rows)
               + [pltpu.VMEM((N_HEADS, HEAD_DIM, grp_rows), BF16),
                  pltpu.VMEM((N_HEADS, HEAD_DIM, grp_rows), F32)])
    return pl.pallas_call(
        _sample_kernel,
        grid=grid,
        in_specs=in_specs,
        out_specs=out_specs,
        out_shape=out_shape,
        scratch_shapes=scratch,
        compiler_params=pltpu.CompilerParams(
            dimension_semantics=("arbitrary", "arbitrary", "arbitrary"),
            vmem_limit_bytes=VMEM_LIMIT_BYTES),
        name="sample_trunk",
    )(x, ada, conv_in, *params, *weights, state)


def _prep_kernel(c_ref, wada_ref, bada_ref, lbp_ref, ada_ref, lbs_ref):
    c = c_ref[...]
    sc = (c * _sigmoid(c)).astype(BF16)
    ada_ref[0] = jnp.dot(sc, wada_ref[0].astype(BF16), preferred_element_type=F32) + bada_ref[0]
    p = [lbp_ref[i:i + 1, :] for i in range(DEPTH)]
    mx = functools.reduce(jnp.maximum, p)
    e = [jnp.exp(pi - mx) for pi in p]
    tot = functools.reduce(lambda a, b: a + b, e)
    acc = jnp.zeros_like(mx)
    lbs_ref[0:1, :] = acc
    for i in range(1, DEPTH):
        acc = acc + e[i] / tot
        lbs_ref[i:i + 1, :] = acc


def _prep(c_all, w_ada, b_ada, lb_param):
    n = c_all.shape[0]
    return pl.pallas_call(
        _prep_kernel,
        grid=(DEPTH,),
        in_specs=[
            pl.BlockSpec((n, D_MODEL), lambda l: (0, 0)),
            pl.BlockSpec((1, D_MODEL, 3 * D_MODEL), lambda l: (l, 0, 0)),
            pl.BlockSpec((1, 1, 3 * D_MODEL), lambda l: (l, 0, 0)),
            pl.BlockSpec((DEPTH, D_MODEL), lambda l: (0, 0)),
        ],
        out_specs=(
            pl.BlockSpec((1, n, 3 * D_MODEL), lambda l: (l, 0, 0)),
            pl.BlockSpec((DEPTH, D_MODEL), lambda l: (0, 0)),
        ),
        out_shape=(
            jax.ShapeDtypeStruct((DEPTH, n, 3 * D_MODEL), F32),
            jax.ShapeDtypeStruct((DEPTH, D_MODEL), F32),
        ),
        compiler_params=pltpu.CompilerParams(dimension_semantics=("arbitrary",)),
        name="conditioning",
    )(c_all, w_ada, b_ada.reshape(DEPTH, 1, 3 * D_MODEL), lb_param)


def _tiled(p):
    return p.reshape(DEPTH, p.shape[1], N_TILES, LANE_TILE).transpose(0, 2, 1, 3)


def kernel(x_prompt, x_sample, state_hgrn, state_conv, c_prompt, c_sample, w_ada, b_ada, w_in, hgrn_lb,
           hgrn_gnorm, conv_w, w_a_out, w_b_out, w_o, ln_g, ln_b):
    n_prompt = x_prompt.shape[0]
    n_seq, t_dec, _ = x_sample.shape
    assert t_dec == SAMPLE_T and n_seq % SAMPLE_GROUP_SEQ == 0 and x_prompt.shape[1] % CHUNK == 0
    n_grp = n_seq // SAMPLE_GROUP_SEQ

    c_all = jnp.concatenate([c_prompt, c_sample], axis=0)
    ada_all, lbs = _prep(c_all, w_ada, b_ada, hgrn_lb)

    params = (_tiled(lbs[:, None, :]), _tiled(hgrn_gnorm[:, None, :]), _tiled(conv_w),
              ln_g[:, None, :], ln_b[:, None, :])
    weights = (w_in.astype(BF16),
               w_a_out.astype(BF16).reshape(DEPTH, N_TILES, LANE_TILE, D_MODEL),
               w_b_out.astype(BF16).reshape(DEPTH, N_TILES, LANE_TILE, D_MODEL),
               w_o.astype(BF16).reshape(DEPTH, N_TILES, LANE_TILE, D_MODEL))

    xp = x_prompt
    hgrn_p, conv_p = [], []
    for l in range(DEPTH):
        ada_p = ada_all[l, :n_prompt].reshape(n_prompt, 1, 3 * D_MODEL)
        xp, s_new, c_new = _prompt_layer(l, xp, ada_p, params, weights)
        hgrn_p.append(s_new)
        conv_p.append(c_new)

    xs = (x_sample.reshape(n_grp, SAMPLE_GROUP_SEQ, t_dec, D_MODEL).transpose(0, 2, 1, 3)
          .reshape(n_seq * t_dec, D_MODEL))
    ys, hgrn_s, conv_s = _sample_trunk(xs, ada_all[:, n_prompt:], state_conv.transpose(0, 2, 1, 3),
                                       params, weights, state_hgrn)
    ys = (ys[DEPTH - 1].reshape(n_grp, t_dec, SAMPLE_GROUP_SEQ, D_MODEL).transpose(0, 2, 1, 3)
          .reshape(n_seq, t_dec, D_MODEL))

    return (xp, ys, jnp.stack(hgrn_p), jnp.stack(conv_p), hgrn_s, conv_s.transpose(0, 2, 1, 3))
```

```python
import functools
import itertools

import jax
import jax.numpy as jnp
from jax import lax
from jax.experimental import pallas as pl
from jax.experimental.pallas import tpu as pltpu

F32 = jnp.float32
BF16 = jnp.bfloat16

D_MODEL = 1024
N_HEADS = 8
HEAD_DIM = 128
N_GROUPS = 10
LANE_TILE = 256
HEADS_PER_TILE = LANE_TILE // HEAD_DIM
N_TILES = D_MODEL // LANE_TILE
CONV_W = 3
CHUNK = 64
DEPTH = 4
DEEPNORM_ALPHA = (2 * DEPTH) ** 0.25
LN_EPS = 1e-5
RMS_EPS = 1e-6
LB_FLOOR = 1e-30
Q_SCALE = HEAD_DIM ** -0.5
LOG2_E = 1.4426950408889634
TREE_LEVELS = (1, 2, 4, 8, 16, 32)
PROMPT_BLOCK_T = 512
SAMPLE_T = 4
SAMPLE_GROUP_SEQ = 32
SAMPLE_SEQ_PER_STEP = 8
VMEM_LIMIT_BYTES = 64 * 1024 * 1024 - 256 * 1024

_TN = (((0,), (0,)), ((), ()))


def _sigmoid(x):
    return 1.0 / (1.0 + jnp.exp2(x * (-LOG2_E)))


def _roll_rows(x, shift):
    return pltpu.roll(x, shift % x.shape[0], axis=0)


def _roll_in_groups(x, shift):
    n, w = x.shape
    return pltpu.roll(x.reshape(n // 8, 8, w), shift % 8, axis=1).reshape(n, w)


def _halves(x, h):
    n = x.shape[0]
    first = [x[b:b + h] for b in range(0, n, 2 * h)]
    second = [x[b + h:b + 2 * h] for b in range(0, n, 2 * h)]
    return first, second


def _interleave(first, second):
    out = []
    for a, b in zip(first, second):
        out += [a, b]
    return jnp.concatenate(out, axis=0)


def _tree_factors(q, k, lf, levels, row):
    p = lf
    tot = lf
    xs = []
    for h in levels:
        if h < 8:
            second = (row & h) != 0
            other = _roll_in_groups(tot, h)
            if h != 4:
                other = jnp.where(second, other, _roll_in_groups(tot, -h))
            x = jnp.where(second, q, k) * jnp.exp2(jnp.where(second, p, tot - p))
            p = p + jnp.where(second, other, 0.0)
            tot = tot + other
        else:
            p1, p2 = _halves(p, h)
            t1, t2 = _halves(tot, h)
            k1, _ = _halves(k, h)
            _, q2 = _halves(q, h)
            x = _interleave([kk * jnp.exp2(tt - pp) for kk, tt, pp in zip(k1, t1, p1)],
                            [qq * jnp.exp2(pp) for qq, pp in zip(q2, p2)])
            p = _interleave(p1, [pp + tt for pp, tt in zip(p2, t1)])
            both = [a + b for a, b in zip(t1, t2)]
            tot = _interleave(both, both)
        xs.append(x.astype(BF16))
    return xs, p, tot


def _pair_level_code(n, levels):
    t = lax.broadcasted_iota(jnp.int32, (n, 2 * n), 0)
    s = lax.broadcasted_iota(jnp.int32, (n, 2 * n), 1) & (n - 1)
    x = t ^ s
    code = jnp.where(s == t, 0, -1)
    for i, h in enumerate(levels):
        code = jnp.where((s < t) & (x >= h) & (x < 2 * h), i + 1, code)
    return code


def _block_diag(x):
    w = x.shape[1] // 2
    z = jnp.zeros((x.shape[0], w), x.dtype)
    return jnp.concatenate([jnp.concatenate([x[:, :w], z], axis=1),
                            jnp.concatenate([z, x[:, w:]], axis=1)], axis=0)


def _block_diag_t(x):
    n = x.shape[0]
    w = x.shape[1] // 2
    z = jnp.zeros((w - n, w), x.dtype)
    t0 = jnp.concatenate([x[:, :w], z], axis=0).T
    t1 = jnp.concatenate([z, x[:, w:]], axis=0).T
    return jnp.concatenate([t0, t1], axis=0)


def _pair_grams(xs):
    return [jnp.dot(x, _block_diag_t(x), preferred_element_type=F32) for x in xs]


def _pair_scores(grams, levels, qk, code):
    n = qk.shape[0]
    w = qk.shape[1] // 2
    lane = lax.broadcasted_iota(jnp.int32, (1, 2 * n), 1)
    diag = jnp.where(lane < n, jnp.sum(qk[:, :w], axis=1, keepdims=True),
                     jnp.sum(qk[:, w:], axis=1, keepdims=True))
    pieces = []
    for r0 in range(0, n, 8):
        rows = slice(r0, r0 + 8)
        c = code[rows]
        a = jnp.where(c == 0, diag[rows], 0.0)
        for i, h in enumerate(levels):
            if h < 8 or (r0 & h):
                a = jnp.where(c == i + 1, grams[i][rows], a)
        pieces.append(a)
    return jnp.concatenate(pieces, axis=0)


def _proj(hb, win_ref, g, j):
    col = g * D_MODEL + j * LANE_TILE
    return jnp.dot(hb, win_ref[:, col:col + LANE_TILE], preferred_element_type=F32)


def _project_recurrent(j, hb, win_ref, lb_ref, q_s, k_s, lf_s, v_s):
    qv = _proj(hb, win_ref, 0, j)
    q_s[j] = qv * _sigmoid(qv) * Q_SCALE
    lb = lb_ref[j]
    one_m = 1.0 - lb
    sig = _sigmoid(_proj(hb, win_ref, 1, j))
    lf_s[j] = jnp.log(jnp.maximum(lb, LB_FLOOR) + one_m * sig) * LOG2_E
    k_s[j] = one_m * (1.0 - sig)
    v_s[j] = _proj(hb, win_ref, 2, j).astype(BF16)


def _project_gates(j, hb, win_ref, cw_ref, delayed, ga_s, yb_s, ra_s, rb_s, u_s):
    proj = functools.partial(_proj, hb, win_ref, j=j)
    ga = proj(3)
    ga_s[j] = (ga * _sigmoid(ga)).astype(BF16)
    yield ga
    cg = proj(5)
    yield cg
    hv = proj(6)
    u = cg * hv
    cw = cw_ref[j]
    u1, u2 = delayed(u)
    conv = u2 * cw[0:1, :] + u1 * cw[1:2, :] + u * cw[2:3, :]
    u_s[j] = u
    yield hv
    bg = proj(4)
    conv = conv * bg
    yield bg
    gb = proj(7)
    yb_s[j] = (conv * (gb * _sigmoid(gb))).astype(BF16)
    yield gb
    ra = proj(8)
    ra_s[j] = _sigmoid(ra).astype(BF16)
    yield ra
    rb = proj(9)
    rb_s[j] = _sigmoid(rb).astype(BF16)
    yield rb


def _output_phase(x, gate, o_s, ga_s, yb_s, ra_s, rb_s, gn_ref, wa_ref, wb_ref, wo_ref, lng_ref, lnb_ref,
                  row_blocks=1):
    if row_blocks > 1:
        step = x.shape[0] // row_blocks
        outs = []
        for r0 in range(0, x.shape[0], step):
            rows = slice(r0, r0 + step)
            view = lambda ref: ref.at[:, rows, :]
            outs.append(_output_phase(x[rows], gate if gate.shape[0] == 1 else gate[rows],
                                      view(o_s), view(ga_s), view(yb_s), view(ra_s), view(rb_s),
                                      gn_ref, wa_ref, wb_ref, wo_ref, lng_ref, lnb_ref))
        return jnp.concatenate(outs, axis=0)
    ya = None
    yb = None
    for j in range(N_TILES):
        o = o_s[j]
        parts = []
        for hh in range(HEADS_PER_TILE):
            oh = o[:, hh * HEAD_DIM:(hh + 1) * HEAD_DIM]
            ms = jnp.mean(oh * oh, axis=1, keepdims=True)
            parts.append(oh * lax.rsqrt(ms + RMS_EPS))
        on = jnp.concatenate(parts, axis=1) * gn_ref[j]
        da = jnp.dot((on * ga_s[j]).astype(BF16), wa_ref[j], preferred_element_type=F32)
        db = jnp.dot(yb_s[j], wb_ref[j], preferred_element_type=F32)
        ya = da if ya is None else ya + da
        yb = db if yb is None else yb + db
    out = None
    for j in range(N_TILES):
        lanes = slice(j * LANE_TILE, (j + 1) * LANE_TILE)
        m = ra_s[j] * ya[:, lanes] + rb_s[j] * yb[:, lanes]
        dm = jnp.dot(m.astype(BF16), wo_ref[j], preferred_element_type=F32)
        out = dm if out is None else out + dm
    res = DEEPNORM_ALPHA * x + gate * out
    mu = jnp.mean(res, axis=1, keepdims=True)
    cen = res - mu
    var = jnp.mean(cen * cen, axis=1, keepdims=True)
    return cen * lax.rsqrt(var + LN_EPS) * lng_ref[...] + lnb_ref[...]


def _modulated_input(x, ada):
    shift = ada[:, 0:D_MODEL]
    scale = ada[:, D_MODEL:2 * D_MODEL]
    gate = ada[:, 2 * D_MODEL:3 * D_MODEL]
    return x * (1.0 + scale) + shift, gate


def _prompt_layer_kernel(x_ref, ada_ref, lb_ref, gn_ref, cw_ref, lng_ref, lnb_ref,
                         win_ref, wa_ref, wb_ref, wo_ref,
                         y_ref, sfin_ref, cfin_ref,
                         hb_s, q_s, k_s, lf_s, v_s, ga_s, yb_s, ra_s, rb_s, u_s, o_s, st_s, cprev_s):
    t_blk = pl.program_id(1)
    tb = x_ref.shape[1]

    @pl.when(t_blk == 0)
    def _():
        st_s[...] = jnp.zeros_like(st_s)
        cprev_s[...] = jnp.zeros_like(cprev_s)

    x = x_ref[0]
    h, gate = _modulated_input(x, ada_ref[0])
    hb_s[...] = h.astype(BF16)

    rows_tb = lax.broadcasted_iota(jnp.int32, (tb, 1), 0)

    for j in range(N_TILES):
        _project_recurrent(j, hb_s[...], win_ref.at[0], lb_ref.at[0], q_s, k_s, lf_s, v_s)

    row = lax.broadcasted_iota(jnp.int32, (CHUNK, 1), 0)
    code = _pair_level_code(CHUNK, TREE_LEVELS)

    def gates(j):
        prev = cprev_s[j]

        def delayed(u):
            u1 = jnp.where(rows_tb == 0, prev[7:8, :], _roll_rows(u, 1))
            u2 = jnp.where(rows_tb == 0, prev[6:7, :],
                           jnp.where(rows_tb == 1, prev[7:8, :], _roll_rows(u, 2)))
            return u1, u2

        yield from _project_gates(j, hb_s[...], win_ref.at[0], cw_ref.at[0], delayed,
                                  ga_s, yb_s, ra_s, rb_s, u_s)
        cprev_s[j] = u_s[j, tb - 8:tb, :]

    filler = itertools.chain.from_iterable(gates(j) for j in range(N_TILES))

    def stage_front(c, j):
        rows = pl.ds(c * CHUNK, CHUNK)
        q = q_s[j, rows, :]
        k = k_s[j, rows, :]
        v = v_s[j, rows, :]
        xs, gc, tot = _tree_factors(q, k, lf_s[j, rows, :], TREE_LEVELS, row)
        grams = _pair_grams(xs)
        next(filler, None)
        return rows, j, q, k, v, gc, tot, grams

    def stage_back(rows, j, q, k, v, gc, tot, grams):
        dec = jnp.exp2(tot[0:1, :])
        qg = (q * jnp.exp2(gc)).astype(BF16)
        kd = (k * jnp.exp2(tot - gc)).astype(BF16)
        a = _pair_scores(grams, TREE_LEVELS, q * k, code)
        states = [st_s[j * HEADS_PER_TILE + hh] for hh in range(HEADS_PER_TILE)]
        zero = jnp.zeros((HEAD_DIM, HEAD_DIM), BF16)
        state_bd = jnp.concatenate(
            [jnp.concatenate([states[0].astype(BF16), zero], axis=1),
             jnp.concatenate([zero, states[1].astype(BF16)], axis=1)], axis=0)
        o = jnp.dot(jnp.concatenate([a.astype(BF16), qg], axis=1),
                    jnp.concatenate([_block_diag(v), state_bd], axis=0), preferred_element_type=F32)
        for hh in range(HEADS_PER_TILE):
            lanes = slice(hh * HEAD_DIM, (hh + 1) * HEAD_DIM)
            dec_col = jnp.broadcast_to(dec[:, lanes], (HEAD_DIM, HEAD_DIM)).T
            st_s[j * HEADS_PER_TILE + hh] = states[hh] * dec_col + lax.dot_general(
                kd[:, lanes], v[:, lanes], _TN, preferred_element_type=F32)
        next(filler, None)
        o_s[j, rows, :] = o

    pending = None
    for c in range(tb // CHUNK):
        for j in range(N_TILES):
            front = stage_front(c, j)
            if pending is not None:
                stage_back(*pending)
            pending = front
    stage_back(*pending)
    for _ in filler:
        pass

    y_ref[0] = _output_phase(x, gate, o_s, ga_s, yb_s, ra_s, rb_s, gn_ref.at[0], wa_ref.at[0],
                             wb_ref.at[0], wo_ref.at[0], lng_ref.at[0], lnb_ref.at[0], row_blocks=2)

    @pl.when(t_blk == pl.num_programs(1) - 1)
    def _():
        sfin_ref[0] = st_s[...]
        for j in range(N_TILES):
            cfin_ref[0, :, j * LANE_TILE:(j + 1) * LANE_TILE] = cprev_s[j, 8 - (CONV_W - 1):8, :]


def _layer_spec(arr, index_of_layer):
    nd = arr.ndim
    return pl.BlockSpec((1,) + arr.shape[1:], lambda *ids: (index_of_layer(*ids),) + (0,) * (nd - 1),
                        pipeline_mode=pl.Buffered(1))


def _tile_scratch(rows):
    f = lambda dt: pltpu.VMEM((N_TILES, rows, LANE_TILE), dt)
    return [f(F32), f(F32), f(F32), f(BF16), f(BF16), f(BF16), f(BF16), f(BF16), f(F32), f(F32)]


def _prompt_layer(layer, x, ada, params, weights):
    bsz, seq, _ = x.shape
    block_t = min(PROMPT_BLOCK_T, seq)
    grid = (bsz, seq // block_t)
    out_shape = (
        jax.ShapeDtypeStruct((bsz, seq, D_MODEL), F32),
        jax.ShapeDtypeStruct((bsz, N_HEADS, HEAD_DIM, HEAD_DIM), F32),
        jax.ShapeDtypeStruct((bsz, CONV_W - 1, D_MODEL), F32),
    )
    at_layer = lambda b, t: layer
    in_specs = ([pl.BlockSpec((1, block_t, D_MODEL), lambda b, t: (b, t, 0)),
                 pl.BlockSpec((1, 1, 3 * D_MODEL), lambda b, t: (b, 0, 0))]
                + [_layer_spec(p, at_layer) for p in params]
                + [_layer_spec(w, at_layer) for w in weights])
    out_specs = (
        pl.BlockSpec((1, block_t, D_MODEL), lambda b, t: (b, t, 0)),
        pl.BlockSpec((1, N_HEADS, HEAD_DIM, HEAD_DIM), lambda b, t: (b, 0, 0, 0)),
        pl.BlockSpec((1, CONV_W - 1, D_MODEL), lambda b, t: (b, 0, 0)),
    )
    scratch = ([pltpu.VMEM((block_t, D_MODEL), BF16)] + _tile_scratch(block_t)
               + [pltpu.VMEM((N_HEADS, HEAD_DIM, HEAD_DIM), F32),
                  pltpu.VMEM((N_TILES, 8, LANE_TILE), F32)])
    return pl.pallas_call(
        _prompt_layer_kernel,
        grid=grid,
        in_specs=in_specs,
        out_specs=out_specs,
        out_shape=out_shape,
        scratch_shapes=scratch,
        compiler_params=pltpu.CompilerParams(
            dimension_semantics=("arbitrary", "arbitrary"),
            vmem_limit_bytes=VMEM_LIMIT_BYTES,
        ),
        name="prompt_layer",
    )(x, ada, *params, *weights)


def _sample_kernel(x_ref, ada_ref, cin_ref, lb_ref, gn_ref, cw_ref, lng_ref, lnb_ref,
                   win_ref, wa_ref, wb_ref, wo_ref, sin_ref,
                   y_ref, sout_ref, cout_ref,
                   xcur_s, hb_s, q_s, k_s, lf_s, v_s, ga_s, yb_s, ra_s, rb_s, u_s, o_s, kdt_s, dct_s):
    layer = pl.program_id(0)
    grp = pl.program_id(1)
    step = pl.program_id(2)
    nseq = SAMPLE_GROUP_SEQ
    n_rows = nseq * SAMPLE_T
    grp_rows = pl.ds(pl.multiple_of(grp * n_rows, n_rows), n_rows)
    tok = lambda t: slice(t * nseq, (t + 1) * nseq)

    @pl.when(step == 0)
    def _():
        @pl.when(layer == 0)
        def _():
            xcur_s[grp_rows, :] = x_ref[...]

        ada = ada_ref[0]
        x = xcur_s[grp_rows, :]
        for t in range(SAMPLE_T):
            h, _ = _modulated_input(x[tok(t)], ada)
            hb_s[tok(t), :] = h.astype(BF16)

        for j in range(N_TILES):
            lanes_j = slice(j * LANE_TILE, (j + 1) * LANE_TILE)
            hist0 = cin_ref[0, 0, :, lanes_j]
            hist1 = cin_ref[0, 1, :, lanes_j]

            def delayed(u):
                return (jnp.concatenate([hist1, u[:(SAMPLE_T - 1) * nseq]], axis=0),
                        jnp.concatenate([hist0, hist1, u[:(SAMPLE_T - 2) * nseq]], axis=0))

            _project_recurrent(j, hb_s[...], win_ref.at[0], lb_ref.at[0], q_s, k_s, lf_s, v_s)
            for _ in _project_gates(j, hb_s[...], win_ref.at[0], cw_ref.at[0], delayed,
                                    ga_s, yb_s, ra_s, rb_s, u_s):
                pass

            q = [q_s[j, tok(t), :] for t in range(SAMPLE_T)]
            k = [k_s[j, tok(t), :] for t in range(SAMPLE_T)]
            v = [v_s[j, tok(t), :].astype(F32) for t in range(SAMPLE_T)]
            gc = [lf_s[j, tok(0), :]]
            for t in range(1, SAMPLE_T):
                gc.append(gc[-1] + lf_s[j, tok(t), :])
            tot = gc[-1]
            for t in range(SAMPLE_T):
                o_heads = [jnp.zeros((nseq, HEAD_DIM), F32) for _ in range(HEADS_PER_TILE)]
                for s in range(t + 1):
                    w = q[t] * k[s]
                    if s < t:
                        w = w * jnp.exp2(gc[t] - gc[s])
                    for hh in range(HEADS_PER_TILE):
                        lanes = slice(hh * HEAD_DIM, (hh + 1) * HEAD_DIM)
                        a = jnp.sum(w[:, lanes], axis=1, keepdims=True)
                        o_heads[hh] = o_heads[hh] + a * v[s][:, lanes]
                o_s[j, tok(t), :] = jnp.concatenate(o_heads, axis=1)
            for t in range(SAMPLE_T):
                q_s[j, tok(t), :] = q[t] * jnp.exp2(gc[t])
                k_s[j, tok(t), :] = k[t] * jnp.exp2(tot - gc[t])
                lf_s[j, tok(t), :] = jnp.exp2(tot)
            for hh in range(HEADS_PER_TILE):
                lanes = slice(hh * HEAD_DIM, (hh + 1) * HEAD_DIM)
                hd = j * HEADS_PER_TILE + hh
                kdt_s[hd] = k_s[j, :, lanes].T.astype(BF16)
                dct_s[hd] = lf_s[j, :, lanes].T

    seq0 = step * SAMPLE_SEQ_PER_STEP
    piece0 = pl.multiple_of((seq0 // 8) * 8, 8)
    lane_id = lax.broadcasted_iota(jnp.int32, (1, n_rows), 1)
    sub_id = lax.broadcasted_iota(jnp.int32, (8 * SAMPLE_T, 1), 0) & 7

    def seq_state(j, carry):
        for hh in range(HEADS_PER_TILE):
            lanes = slice(hh * HEAD_DIM, (hh + 1) * HEAD_DIM)
            hd = j * HEADS_PER_TILE + hh
            qg = jnp.concatenate([q_s[j, pl.ds(t * nseq + piece0, 8), lanes] for t in range(SAMPLE_T)],
                                 axis=0).astype(BF16)
            vblk = v_s[j, :, lanes]
            kdt = kdt_s[hd]
            dct = dct_s[hd]
            o_acc = jnp.zeros((8 * SAMPLE_T, HEAD_DIM), F32)
            for s in range(SAMPLE_SEQ_PER_STEP):
                seq = seq0 + s
                st = sin_ref[0, s, hd]
                res = jnp.dot(qg, st.astype(BF16), preferred_element_type=F32)
                o_acc = o_acc + jnp.where(sub_id == (seq & 7), res, 0.0)
                mine = (lane_id & (nseq - 1)) == seq
                upd = jnp.dot(jnp.where(mine, kdt, jnp.zeros_like(kdt)), vblk, preferred_element_type=F32)
                dcol = jnp.sum(jnp.where(lane_id[:, :nseq] == seq, dct[:, :nseq], 0.0), axis=1, keepdims=True)
                sout_ref[0, s, hd] = st * dcol + upd
            for t in range(SAMPLE_T):
                rows = pl.ds(t * nseq + piece0, 8)
                o_s[j, rows, lanes] = o_s[j, rows, lanes] + o_acc[8 * t:8 * (t + 1)]
        return carry

    lax.fori_loop(0, N_TILES, seq_state, 0)

    @pl.when(step == pl.num_programs(2) - 1)
    def _():
        ada = ada_ref[0]
        gate = jnp.concatenate([ada[:, 2 * D_MODEL:3 * D_MODEL]] * SAMPLE_T, axis=0)
        y = _output_phase(xcur_s[grp_rows, :], gate, o_s, ga_s, yb_s, ra_s, rb_s, gn_ref.at[0],
                          wa_ref.at[0], wb_ref.at[0], wo_ref.at[0], lng_ref.at[0], lnb_ref.at[0])
        y_ref[0] = y
        xcur_s[grp_rows, :] = y
        for j in range(N_TILES):
            for i in range(CONV_W - 1):
                cout_ref[0, i, :, j * LANE_TILE:(j + 1) * LANE_TILE] = u_s[j, tok(SAMPLE_T - (CONV_W - 1) + i), :]


def _sample_trunk(x, ada, conv_in, params, weights, state):
    n_rows_all = x.shape[0]
    n_seq = state.shape[1]
    n_grp = n_seq // SAMPLE_GROUP_SEQ
    grp_rows = SAMPLE_GROUP_SEQ * SAMPLE_T
    steps = SAMPLE_GROUP_SEQ // SAMPLE_SEQ_PER_STEP
    grid = (DEPTH, n_grp, steps)
    state_blk = (1, SAMPLE_SEQ_PER_STEP, N_HEADS, HEAD_DIM, HEAD_DIM)
    state_idx = lambda l, g, s: (l, g * steps + s, 0, 0, 0)
    conv_blk = (1, CONV_W - 1, SAMPLE_GROUP_SEQ, D_MODEL)
    conv_idx = lambda l, g, s: (l, 0, g, 0)
    at_layer = lambda l, g, s: l
    out_shape = (
        jax.ShapeDtypeStruct((DEPTH, n_rows_all, D_MODEL), F32),
        jax.ShapeDtypeStruct(state.shape, F32),
        jax.ShapeDtypeStruct(conv_in.shape, F32),
    )
    in_specs = ([pl.BlockSpec((grp_rows, D_MODEL), lambda l, g, s: (g, 0)),
                 pl.BlockSpec((1, SAMPLE_GROUP_SEQ, 3 * D_MODEL), lambda l, g, s: (l, g, 0)),
                 pl.BlockSpec(conv_blk, conv_idx)]
                + [_layer_spec(p, at_layer) for p in params]
                + [_layer_spec(w, at_layer) for w in weights]
                + [pl.BlockSpec(state_blk, state_idx)])
    out_specs = (
        pl.BlockSpec((1, grp_rows, D_MODEL), lambda l, g, s: (l, g, 0)),
        pl.BlockSpec(state_blk, state_idx),
        pl.BlockSpec(conv_blk, conv_idx),
    )
    scratch = ([pltpu.VMEM((n_rows_all, D_MODEL), F32), pltpu.VMEM((grp_rows, D_MODEL), BF16)]
               + _tile_scratch(grp_rows)
               + [pltpu.VMEM((N_HEADS, HEAD_DIM, grp_rows), BF16),
                  pltpu.VMEM((N_HEADS, HEAD_DIM, grp_rows), F32)])
    return pl.pallas_call(
        _sample_kernel,
        grid=grid,
        in_specs=in_specs,
        out_specs=out_specs,
        out_shape=out_shape,
        scratch_shapes=scratch,
        compiler_params=pltpu.CompilerParams(
            dimension_semantics=("arbitrary", "arbitrary", "arbitrary"),
            vmem_limit_bytes=VMEM_LIMIT_BYTES),
        name="sample_trunk",
    )(x, ada, conv_in, *params, *weights, state)


def _prep_kernel(c_ref, wada_ref, bada_ref, lbp_ref, ada_ref, lbs_ref):
    c = c_ref[...]
    sc = (c * _sigmoid(c)).astype(BF16)
    ada_ref[0] = jnp.dot(sc, wada_ref[0].astype(BF16), preferred_element_type=F32) + bada_ref[0]
    p = [lbp_ref[i:i + 1, :] for i in range(DEPTH)]
    mx = functools.reduce(jnp.maximum, p)
    e = [jnp.exp(pi - mx) for pi in p]
    tot = functools.reduce(lambda a, b: a + b, e)
    acc = jnp.zeros_like(mx)
    lbs_ref[0:1, :] = acc
    for i in range(1, DEPTH):
        acc = acc + e[i] / tot
        lbs_ref[i:i + 1, :] = acc


def _prep(c_all, w_ada, b_ada, lb_param):
    n = c_all.shape[0]
    return pl.pallas_call(
        _prep_kernel,
        grid=(DEPTH,),
        in_specs=[
            pl.BlockSpec((n, D_MODEL), lambda l: (0, 0)),
            pl.BlockSpec((1, D_MODEL, 3 * D_MODEL), lambda l: (l, 0, 0)),
            pl.BlockSpec((1, 1, 3 * D_MODEL), lambda l: (l, 0, 0)),
            pl.BlockSpec((DEPTH, D_MODEL), lambda l: (0, 0)),
        ],
        out_specs=(
            pl.BlockSpec((1, n, 3 * D_MODEL), lambda l: (l, 0, 0)),
            pl.BlockSpec((DEPTH, D_MODEL), lambda l: (0, 0)),
        ),
        out_shape=(
            jax.ShapeDtypeStruct((DEPTH, n, 3 * D_MODEL), F32),
            jax.ShapeDtypeStruct((DEPTH, D_MODEL), F32),
        ),
        compiler_params=pltpu.CompilerParams(dimension_semantics=("arbitrary",)),
        name="conditioning",
    )(c_all, w_ada, b_ada.reshape(DEPTH, 1, 3 * D_MODEL), lb_param)


def _tiled(p):
    return p.reshape(DEPTH, p.shape[1], N_TILES, LANE_TILE).transpose(0, 2, 1, 3)


def kernel(x_prompt, x_sample, state_hgrn, state_conv, c_prompt, c_sample, w_ada, b_ada, w_in, hgrn_lb,
           hgrn_gnorm, conv_w, w_a_out, w_b_out, w_o, ln_g, ln_b):
    n_prompt = x_prompt.shape[0]
    n_seq, t_dec, _ = x_sample.shape
    assert t_dec == SAMPLE_T and n_seq % SAMPLE_GROUP_SEQ == 0 and x_prompt.shape[1] % CHUNK == 0
    n_grp = n_seq // SAMPLE_GROUP_SEQ

    c_all = jnp.concatenate([c_prompt, c_sample], axis=0)
    ada_all, lbs = _prep(c_all, w_ada, b_ada, hgrn_lb)

    params = (_tiled(lbs[:, None, :]), _tiled(hgrn_gnorm[:, None, :]), _tiled(conv_w),
              ln_g[:, None, :], ln_b[:, None, :])
    weights = (w_in.astype(BF16),
               w_a_out.astype(BF16).reshape(DEPTH, N_TILES, LANE_TILE, D_MODEL),
               w_b_out.astype(BF16).reshape(DEPTH, N_TILES, LANE_TILE, D_MODEL),
               w_o.astype(BF16).reshape(DEPTH, N_TILES, LANE_TILE, D_MODEL))

    xp = x_prompt
    hgrn_p, conv_p = [], []
    for l in range(DEPTH):
        ada_p = ada_all[l, :n_prompt].reshape(n_prompt, 1, 3 * D_MODEL)
        xp, s_new, c_new = _prompt_layer(l, xp, ada_p, params, weights)
        hgrn_p.append(s_new)
        conv_p.append(c_new)

    xs = (x_sample.reshape(n_grp, SAMPLE_GROUP_SEQ, t_dec, D_MODEL).transpose(0, 2, 1, 3)
          .reshape(n_seq * t_dec, D_MODEL))
    ys, hgrn_s, conv_s = _sample_trunk(xs, ada_all[:, n_prompt:], state_conv.transpose(0, 2, 1, 3),
                                       params, weights, state_hgrn)
    ys = (ys[DEPTH - 1].reshape(n_grp, t_dec, SAMPLE_GROUP_SEQ, D_MODEL).transpose(0, 2, 1, 3)
          .reshape(n_seq, t_dec, D_MODEL))

    return (xp, ys, jnp.stack(hgrn_p), jnp.stack(conv_p), hgrn_s, conv_s.transpose(0, 2, 1, 3))
```
